```python
import jax, jax.numpy as jnp
from jax import lax
import numpy as np

D_MODEL = 1024
BATCH = 2
SEQ = 8192
DEPTH = 1
DEC_BATCH = 128
DEC_SEQ = 1
PAST_LEN = 8192
PAGE_SIZE = 128

D_MIX = D_MODEL
RET_HEADS = 4
RET_DK = 128
RET_DV = 128
RET_QK_WIDTH = RET_HEADS * RET_DK
RET_WIDTH = RET_HEADS * RET_DV
RET_CHUNK = 128
ROPE_BASE = 10000.0
SWA_Q_HEADS = 8
SWA_KV_HEADS = 2
SWA_HD = 64
SWA_GROUP = SWA_Q_HEADS // SWA_KV_HEADS
SWA_WIDTH = SWA_Q_HEADS * SWA_HD
SWA_KV_WIDTH = SWA_KV_HEADS * SWA_HD
WINDOW = 128
SWA_BLOCK = WINDOW
IN_WIDTHS = (RET_QK_WIDTH, RET_QK_WIDTH, RET_WIDTH, RET_WIDTH, SWA_WIDTH, SWA_KV_WIDTH, SWA_KV_WIDTH)
D_IN = sum(IN_WIDTHS)
IN_SPLITS = tuple(int(v) for v in np.cumsum(IN_WIDTHS)[:-1])
PEER_HEADS = 8
PEER_N_KEYS = 128
PEER_N_EXPERTS = PEER_N_KEYS * PEER_N_KEYS
PEER_D_QUERY = 256
PEER_D_HALF = PEER_D_QUERY // 2
PEER_TOPK = 16
PEER_BLOCK = 128
RMS_EPS = 1e-6

kernel_name = "hymba_retention_swa_sink_peer_step"


def rms_norm(x, g):
    xf = x.astype(jnp.float32)
    y = xf * lax.rsqrt(jnp.mean(xf * xf, axis=-1, keepdims=True) + RMS_EPS)
    return (y * g.astype(jnp.float32)).astype(x.dtype)


def rotate_pairs(x, pos):
    half = x.shape[-1] // 2
    inv = 1.0 / (ROPE_BASE ** jnp.linspace(0.0, 1.0, half, dtype=jnp.float32))
    ang = pos.astype(jnp.float32)[:, None] * inv[None, :]
    cos = jnp.cos(ang)[:, None, :].astype(x.dtype)
    sin = jnp.sin(ang)[:, None, :].astype(x.dtype)
    xr = x.reshape(*x.shape[:-1], half, 2)
    x1, x2 = xr[..., 0], xr[..., 1]
    return jnp.stack([x1 * cos - x2 * sin, x1 * sin + x2 * cos], axis=-1).reshape(x.shape)


def project_in(xn, w_in, pos):
    b, l, _ = xn.shape
    qr, kr, vr, gr, qs, ks, vs = jnp.split(xn @ w_in, IN_SPLITS, axis=-1)
    qr = rotate_pairs(qr.reshape(b, l, RET_HEADS, RET_DK), pos)
    kr = rotate_pairs(kr.reshape(b, l, RET_HEADS, RET_DK), pos) * (RET_DK ** -0.5)
    vr = vr.reshape(b, l, RET_HEADS, RET_DV)
    qs = qs.reshape(b, l, SWA_Q_HEADS, SWA_HD)
    ks = ks.reshape(b, l, SWA_KV_HEADS, SWA_HD)
    vs = vs.reshape(b, l, SWA_KV_HEADS, SWA_HD)
    return qr, kr, vr, gr, qs, ks, vs


def ret_log_decay():
    return jnp.log(1.0 - 2.0 ** (-5.0 - jnp.arange(RET_HEADS, dtype=jnp.float32)))


def retention_chunk(q, k, v, state):
    L = q.shape[1]
    lg = ret_log_decay()
    t = jnp.arange(L, dtype=jnp.float32)
    diff = t[:, None] - t[None, :]
    causal = diff >= 0
    dmask = jnp.where(causal[None], jnp.exp(jnp.where(causal, diff, 0.0)[None] * lg[:, None, None]), 0.0)
    scores = jnp.einsum('blhd,bmhd->bhlm', q, k) * dmask.astype(q.dtype)[None]
    o = jnp.einsum('bhlm,bmhe->blhe', scores, v)
    q_decay = jnp.exp((t + 1.0)[:, None] * lg[None, :]).astype(q.dtype)
    o = o + jnp.einsum('blhd,bhde->blhe', q, state) * q_decay[None, :, :, None]
    k_decay = jnp.exp((L - 1.0 - t)[:, None] * lg[None, :]).astype(q.dtype)
    new_state = (state * jnp.exp(L * lg).astype(q.dtype)[None, :, None, None]
                 + jnp.einsum('blhd,blhe->bhde', k * k_decay[None, :, :, None], v))
    return o, new_state


def retention_prompt(q, k, v):
    b, s = q.shape[0], q.shape[1]
    nc = s // RET_CHUNK
    to_chunks = lambda a: jnp.moveaxis(a.reshape(b, nc, RET_CHUNK, *a.shape[2:]), 1, 0)
    s0 = jnp.zeros((b, RET_HEADS, RET_DK, RET_DV), v.dtype)

    def step(st, qkv):
        o, st = retention_chunk(qkv[0], qkv[1], qkv[2], st)
        return st, o

    s_final, o = lax.scan(step, s0, (to_chunks(q), to_chunks(k), to_chunks(v)))
    return jnp.moveaxis(o, 0, 1).reshape(b, s, RET_HEADS, RET_DV), s_final


def retention_output(o, g):
    of = o.astype(jnp.float32)
    of = of * lax.rsqrt(jnp.mean(of * of, axis=-1, keepdims=True) + RMS_EPS)
    o = of.astype(o.dtype).reshape(*o.shape[:2], RET_WIDTH)
    return o * jax.nn.silu(g)


def sink_attention(q, k, v, q_pos, k_pos, sinks):
    qg = q.reshape(*q.shape[:-2], SWA_KV_HEADS, SWA_GROUP, SWA_HD)
    s = jnp.einsum('...qkgd,...skd->...kgqs', qg, k).astype(jnp.float32) * (SWA_HD ** -0.5)
    rel = q_pos[..., :, None] - k_pos[..., None, :]
    mask = (rel >= 0) & (rel < WINDOW) & (k_pos[..., None, :] >= 0)
    s = jnp.where(mask[..., None, None, :, :], s, -jnp.inf)
    sink = sinks.astype(jnp.float32).reshape(SWA_KV_HEADS, SWA_GROUP, 1, 1)
    m = jnp.maximum(jnp.max(s, axis=-1, keepdims=True), sink)
    p = jnp.exp(s - m)
    p = p / (jnp.sum(p, axis=-1, keepdims=True) + jnp.exp(sink - m))
    o = jnp.einsum('...kgqs,...skd->...qkgd', p.astype(v.dtype), v)
    return o.reshape(*q.shape[:-2], SWA_WIDTH)


def swa_prompt(q, k, v, sinks):
    b, s = q.shape[0], q.shape[1]
    nb = s // SWA_BLOCK
    qb = q.reshape(b, nb, SWA_BLOCK, SWA_Q_HEADS, SWA_HD)
    kb = k.reshape(b, nb, SWA_BLOCK, SWA_KV_HEADS, SWA_HD)
    vb = v.reshape(b, nb, SWA_BLOCK, SWA_KV_HEADS, SWA_HD)
    prev = lambda a: jnp.concatenate([jnp.zeros_like(a[:, :1]), a[:, :-1]], axis=1)
    kk = jnp.concatenate([prev(kb), kb], axis=2)
    vv = jnp.concatenate([prev(vb), vb], axis=2)
    pos_b = jnp.arange(s, dtype=jnp.int32).reshape(nb, SWA_BLOCK)
    k_pos = jnp.concatenate([pos_b - SWA_BLOCK, pos_b], axis=1)
    o = sink_attention(qb, kk, vv, pos_b, k_pos, sinks)
    return o.reshape(b, s, SWA_WIDTH)


def peer_block(xt, peer_w_q, peer_sub_keys, peer_u, peer_v):
    t = xt.shape[0]
    q = (xt @ peer_w_q).reshape(t, PEER_HEADS, 2, PEER_D_HALF)
    s = jnp.einsum('thcd,hcnd->thcn', q, peer_sub_keys).astype(jnp.float32)
    sv, si = lax.top_k(s, PEER_TOPK)
    cand = sv[..., 0, :, None] + sv[..., 1, None, :]
    cand_idx = si[..., 0, :, None] * PEER_N_KEYS + si[..., 1, None, :]
    best, sel = lax.top_k(cand.reshape(t, PEER_HEADS, PEER_TOPK * PEER_TOPK), PEER_TOPK)
    idx = jnp.take_along_axis(cand_idx.reshape(t, PEER_HEADS, PEER_TOPK * PEER_TOPK), sel, axis=-1)
    gates = jax.nn.softmax(best, axis=-1).astype(xt.dtype)
    h = jnp.einsum('thkd,td->thk', peer_u[idx], xt)
    a = jax.nn.gelu(h, approximate=False) * gates
    return jnp.einsum('thk,thkd->td', a, peer_v[idx])


def peer_ffn(x, peer_w_q, peer_sub_keys, peer_u, peer_v):
    shape = x.shape
    xt = x.reshape(-1, shape[-1])
    t = xt.shape[0]
    nb = -(-t // PEER_BLOCK)
    xt = jnp.pad(xt, ((0, nb * PEER_BLOCK - t), (0, 0))).reshape(nb, PEER_BLOCK, shape[-1])
    out = lax.map(lambda xb: peer_block(xb, peer_w_q, peer_sub_keys, peer_u, peer_v), xt)
    return out.reshape(-1, shape[-1])[:t].reshape(shape)


def finish_layer(x, o_ret, g_ret, o_swa, w_out, norm2_g, peer_w_q, peer_sub_keys, peer_u, peer_v, final_g):
    h = x + jnp.concatenate([retention_output(o_ret, g_ret), o_swa], axis=-1) @ w_out
    h = h + peer_ffn(rms_norm(h, norm2_g), peer_w_q, peer_sub_keys, peer_u, peer_v)
    return rms_norm(h, final_g)


def setup_inputs(seed: int = 0) -> dict:
    key = jax.random.key(seed)
    ks = jax.random.split(key, 15)
    f = jnp.float32
    nrm = lambda k, shp, sc: jax.random.normal(k, shp, f) * sc
    return {
        "x_prompt": nrm(ks[0], (BATCH, SEQ, D_MODEL), 1.0),
        "x_sample": nrm(ks[1], (DEC_BATCH, DEC_SEQ, D_MODEL), 1.0),
        "state_ret": nrm(ks[2], (DEC_BATCH, RET_HEADS, RET_DK, RET_DV), 1.0),
        "cache_swa_k": nrm(ks[3], (DEC_BATCH, WINDOW, SWA_KV_HEADS, SWA_HD), 1.0),
        "cache_swa_v": nrm(ks[4], (DEC_BATCH, WINDOW, SWA_KV_HEADS, SWA_HD), 1.0),
        "norm1_g": 1.0 + nrm(ks[5], (D_MODEL,), 0.02),
        "w_in": nrm(ks[6], (D_MODEL, D_IN), D_MODEL ** -0.5),
        "swa_sinks": nrm(ks[7], (SWA_Q_HEADS,), 0.5),
        "w_out": nrm(ks[8], (D_MIX, D_MODEL), D_MIX ** -0.5),
        "norm2_g": 1.0 + nrm(ks[9], (D_MODEL,), 0.02),
        "peer_w_q": nrm(ks[10], (D_MODEL, PEER_HEADS * PEER_D_QUERY), D_MODEL ** -0.5),
        "peer_sub_keys": nrm(ks[11], (PEER_HEADS, 2, PEER_N_KEYS, PEER_D_HALF), PEER_D_HALF ** -0.5),
        "peer_u": nrm(ks[12], (PEER_N_EXPERTS, D_MODEL), D_MODEL ** -0.5),
        "peer_v": nrm(ks[13], (PEER_N_EXPERTS, D_MODEL), PEER_HEADS ** -0.5),
        "final_g": 1.0 + nrm(ks[14], (D_MODEL,), 0.02),
    }


def reference(x_prompt, x_sample, state_ret, cache_swa_k, cache_swa_v, norm1_g, w_in, swa_sinks,
              w_out, norm2_g, peer_w_q, peer_sub_keys, peer_u, peer_v, final_g):
    y_p = x_prompt
    y_s = x_sample
    for _ in range(DEPTH):
        s_len = y_p.shape[1]
        pos_p = jnp.arange(s_len, dtype=jnp.int32)
        qr, kr, vr, gr, qs, ks, vs = project_in(rms_norm(y_p, norm1_g), w_in, pos_p)
        o_ret, state_ret_p = retention_prompt(qr, kr, vr)
        o_swa = swa_prompt(qs, ks, vs, swa_sinks)
        swa_k_p = ks[:, -WINDOW:]
        swa_v_p = vs[:, -WINDOW:]
        y_p = finish_layer(y_p, o_ret, gr, o_swa, w_out, norm2_g, peer_w_q, peer_sub_keys,
                           peer_u, peer_v, final_g)

        d_len = y_s.shape[1]
        pos_s = PAST_LEN + jnp.arange(d_len, dtype=jnp.int32)
        qr, kr, vr, gr, qs, ks, vs = project_in(rms_norm(y_s, norm1_g), w_in, pos_s)
        o_ret, state_ret_s = retention_chunk(qr, kr, vr, state_ret.astype(vr.dtype))
        kk = jnp.concatenate([cache_swa_k.astype(ks.dtype), ks], axis=1)
        vv = jnp.concatenate([cache_swa_v.astype(vs.dtype), vs], axis=1)
        k_pos = jnp.concatenate([PAST_LEN - WINDOW + jnp.arange(WINDOW, dtype=jnp.int32), pos_s])
        o_swa = sink_attention(qs, kk, vv, pos_s, k_pos, swa_sinks)
        swa_k_s = kk[:, -WINDOW:]
        swa_v_s = vv[:, -WINDOW:]
        y_s = finish_layer(y_s, o_ret, gr, o_swa, w_out, norm2_g, peer_w_q, peer_sub_keys,
                           peer_u, peer_v, final_g)
    return (y_p, y_s, state_ret_p, swa_k_p, swa_v_p, state_ret_s, swa_k_s, swa_v_s)
```

```python
import functools
import math

import jax
import jax.numpy as jnp
import numpy as np
from jax import lax
from jax.experimental import pallas as pl
from jax.experimental.pallas import tpu as pltpu

F32 = jnp.float32
BF16 = jnp.bfloat16

D_MODEL = 1024
SEQ = 8192
PAST_LEN = 8192
RET_HEADS = 4
RET_DK = 128
CHUNK = 128
ROPE_BASE = 10000.0
SWA_Q_HEADS = 8
SWA_HD = 64
WINDOW = 128
D_IN = 2816
OFF_RQ, OFF_RK, OFF_RV, OFF_RG, OFF_SQ, OFF_SK, OFF_SV = 0, 512, 1024, 1536, 2048, 2560, 2688
PEER_HEADS = 8
PEER_N_KEYS = 128
PEER_N_EXPERTS = PEER_N_KEYS * PEER_N_KEYS
PEER_TOPK = 16
RMS_EPS = 1e-6
LANES = 128

LOG_DECAY = tuple(math.log(1.0 - 2.0 ** (-5.0 - h)) for h in range(RET_HEADS))

VMEM_LIMIT_BYTES = 56 * 1024 * 1024

NEG_INF = float("-inf")
NT_DIMS = (((1,), (1,)), ((), ()))
TN_DIMS = (((0,), (0,)), ((), ()))


def _rms_norm(x, g):
    return x * lax.rsqrt(jnp.mean(x * x, axis=-1, keepdims=True) + RMS_EPS) * g


def _rotation_tables(pos, inv):
    ang = pos * inv
    cos = jnp.cos(ang)
    sin = jnp.sin(ang)
    lane = lax.broadcasted_iota(jnp.int32, ang.shape, 1)
    even = (lane & 1) == 0
    return cos, jnp.where(even, -sin, sin), even


def _rotate_pairs(x, cos, sin_signed, even):
    partner = jnp.where(even, pltpu.roll(x, LANES - 1, 1), pltpu.roll(x, 1, 1))
    return x * cos + partner * sin_signed


def _silu(g):
    return g * jax.nn.sigmoid(g)


def _head_half_variants(a, lo):
    rolled = pltpu.roll(a, 64, 1)
    zero = jnp.zeros_like(a)
    return (jnp.where(lo, a, zero), jnp.where(lo, zero, rolled),
            jnp.where(lo, rolled, zero), jnp.where(lo, zero, a))


def _prompt_mixer_kernel(x_ref, g1_ref, win_ref, sinks_ref, wout_ref, g2_ref, inv_ref,
                         h_ref, h2t_ref, st_ref, kp_ref, vp_ref,
                         proj_ref, state_ref, kcat_ref, vcat_ref, mix_ref,
                         dmask_ref, qdec_ref, kdec_ref, *, tq):
    s = pl.program_id(1)
    n_chunks = tq // CHUNK

    @pl.when(s == 0)
    def _init():
        state_ref[...] = jnp.zeros_like(state_ref)
        kcat_ref[...] = jnp.zeros_like(kcat_ref)
        vcat_ref[...] = jnp.zeros_like(vcat_ref)
        t_row = lax.broadcasted_iota(jnp.int32, (CHUNK, CHUNK), 0).astype(F32)
        t_col = lax.broadcasted_iota(jnp.int32, (CHUNK, CHUNK), 1).astype(F32)
        diff = t_row - t_col
        causal = diff >= 0
        for hh in range(RET_HEADS):
            lg = LOG_DECAY[hh]
            dmask_ref[hh] = jnp.where(causal, jnp.exp(jnp.where(causal, diff, 0.0) * lg), 0.0)
            qdec_ref[hh] = jnp.exp((t_row + 1.0) * lg)
            kdec_ref[hh] = jnp.exp((CHUNK - 1.0 - t_row) * lg)

    x = x_ref[...]
    xn = _rms_norm(x, g1_ref[...]).astype(BF16)
    proj_ref[...] = jnp.dot(xn, win_ref[...], preferred_element_type=F32)

    def chunk_body(c, carry):
        r0 = pl.multiple_of(c * CHUNK, CHUNK)
        rows = pl.ds(r0, CHUNK)
        chunk_idx = s * n_chunks + c
        t_local = lax.broadcasted_iota(jnp.int32, (CHUNK, 1), 0)
        pos = (chunk_idx * CHUNK + t_local).astype(F32)
        cos, sin_signed, even = _rotation_tables(pos, inv_ref[...])

        for hh in range(RET_HEADS):
            lanes = slice(hh * RET_DK, (hh + 1) * RET_DK)
            q = _rotate_pairs(proj_ref[rows, OFF_RQ + hh * 128:OFF_RQ + (hh + 1) * 128],
                              cos, sin_signed, even)
            k = _rotate_pairs(proj_ref[rows, OFF_RK + hh * 128:OFF_RK + (hh + 1) * 128],
                              cos, sin_signed, even) * (RET_DK ** -0.5)
            v = proj_ref[rows, OFF_RV + hh * 128:OFF_RV + (hh + 1) * 128]
            gate = proj_ref[rows, OFF_RG + hh * 128:OFF_RG + (hh + 1) * 128]
            qb, vb = q.astype(BF16), v.astype(BF16)
            scores = lax.dot_general(qb, k.astype(BF16), NT_DIMS,
                                     preferred_element_type=F32) * dmask_ref[hh]
            state = state_ref[hh]
            o = (jnp.dot(scores.astype(BF16), vb, preferred_element_type=F32)
                 + jnp.dot(qb, state.astype(BF16), preferred_element_type=F32) * qdec_ref[hh])
            k_dec = (k * kdec_ref[hh]).astype(BF16)
            state_ref[hh] = (state * math.exp(CHUNK * LOG_DECAY[hh])
                             + lax.dot_general(k_dec, vb, TN_DIMS, preferred_element_type=F32))
            o = o * lax.rsqrt(jnp.mean(o * o, axis=-1, keepdims=True) + RMS_EPS)
            mix_ref[rows, lanes] = (o * _silu(gate)).astype(BF16)

        kc = proj_ref[rows, OFF_SK:OFF_SK + 128]
        vc = proj_ref[rows, OFF_SV:OFF_SV + 128]
        kp_ref[...] = kc
        vp_ref[...] = vc
        lane = lax.broadcasted_iota(jnp.int32, (CHUNK, LANES), 1)
        lo = lane < 64
        for idx, (kvar, vvar) in enumerate(zip(_head_half_variants(kc, lo),
                                                _head_half_variants(vc, lo))):
            kcat_ref[idx, CHUNK:2 * CHUNK, :] = kvar.astype(BF16)
            vcat_ref[idx, CHUNK:2 * CHUNK, :] = vvar.astype(BF16)
        q_row = lax.broadcasted_iota(jnp.int32, (CHUNK, 2 * CHUNK), 0)
        k_col = lax.broadcasted_iota(jnp.int32, (CHUNK, 2 * CHUNK), 1)
        valid = (k_col >= q_row + 1) & (k_col <= q_row + WINDOW)
        valid = valid & ((k_col >= CHUNK) | (chunk_idx > 0))
        for qb_idx in range(SWA_Q_HEADS // 2):
            g = qb_idx // 2
            q_blk = proj_ref[rows, OFF_SQ + qb_idx * 128:OFF_SQ + (qb_idx + 1) * 128].astype(BF16)
            acc = jnp.zeros((CHUNK, LANES), F32)
            for par in range(2):
                head = 2 * qb_idx + par
                sink = sinks_ref[head:head + 1, :]
                sc = lax.dot_general(q_blk, kcat_ref[g * 2 + par], NT_DIMS,
                                     preferred_element_type=F32) * (SWA_HD ** -0.5)
                sc = jnp.where(valid, sc, NEG_INF)
                m = jnp.maximum(jnp.max(sc, axis=-1, keepdims=True), sink)
                p = jnp.exp(sc - m)
                den = jnp.sum(p, axis=-1, keepdims=True) + jnp.exp(sink - m)
                acc = acc + jnp.dot(p.astype(BF16), vcat_ref[g * 2 + par],
                                    preferred_element_type=F32) / den
            mix_ref[rows, 512 + qb_idx * 128:512 + (qb_idx + 1) * 128] = acc.astype(BF16)
        for idx in range(4):
            kcat_ref[idx, 0:CHUNK, :] = kcat_ref[idx, CHUNK:2 * CHUNK, :]
            vcat_ref[idx, 0:CHUNK, :] = vcat_ref[idx, CHUNK:2 * CHUNK, :]
        return carry

    lax.fori_loop(0, n_chunks, chunk_body, 0)

    h = x + jnp.dot(mix_ref[...], wout_ref[...], preferred_element_type=F32)
    h_ref[...] = h
    h2t_ref[...] = _rms_norm(h, g2_ref[...]).T.astype(BF16)
    st_ref[...] = state_ref[...]


def _prompt_mixer(x, g1, win, sinks, wout, g2, inv, *, tq):
    batch, seq, d = x.shape
    ns = seq // tq
    full = lambda shape: pl.BlockSpec(shape, lambda b, s: (0,) * len(shape))
    return pl.pallas_call(
        functools.partial(_prompt_mixer_kernel, tq=tq),
        grid=(batch, ns),
        in_specs=[
            pl.BlockSpec((None, tq, d), lambda b, s: (b, s, 0)),
            full((1, d)), full((d, D_IN)), full((SWA_Q_HEADS, 1)), full((d, d)), full((1, d)),
            full((1, LANES)),
        ],
        out_specs=[
            pl.BlockSpec((tq, d), lambda b, s: (b * ns + s, 0)),
            pl.BlockSpec((d, tq), lambda b, s: (0, b * ns + s)),
            pl.BlockSpec((None, RET_HEADS, 128, 128), lambda b, s: (b, 0, 0, 0)),
            pl.BlockSpec((None, WINDOW, 128), lambda b, s: (b, 0, 0)),
            pl.BlockSpec((None, WINDOW, 128), lambda b, s: (b, 0, 0)),
        ],
        out_shape=[
            jax.ShapeDtypeStruct((batch * seq, d), F32),
            jax.ShapeDtypeStruct((d, batch * seq), BF16),
            jax.ShapeDtypeStruct((batch, RET_HEADS, 128, 128), F32),
            jax.ShapeDtypeStruct((batch, WINDOW, 128), F32),
            jax.ShapeDtypeStruct((batch, WINDOW, 128), F32),
        ],
        scratch_shapes=[
            pltpu.VMEM((tq, D_IN), F32),
            pltpu.VMEM((RET_HEADS, 128, 128), F32),
            pltpu.VMEM((4, 2 * CHUNK, LANES), BF16),
            pltpu.VMEM((4, 2 * CHUNK, LANES), BF16),
            pltpu.VMEM((tq, d), BF16),
            pltpu.VMEM((RET_HEADS, CHUNK, CHUNK), F32),
            pltpu.VMEM((RET_HEADS, CHUNK, CHUNK), F32),
            pltpu.VMEM((RET_HEADS, CHUNK, CHUNK), F32),
        ],
        compiler_params=pltpu.CompilerParams(
            dimension_semantics=("arbitrary", "arbitrary"),
            vmem_limit_bytes=VMEM_LIMIT_BYTES),
        name="prompt_mixer",
    )(x, g1, win, sinks, wout, g2, inv)


def _sample_proj_kernel(x_ref, g1_ref, win_ref, inv_ref, proj_ref):
    xn = _rms_norm(x_ref[...], g1_ref[...]).astype(BF16)
    proj = jnp.dot(xn, win_ref[...], preferred_element_type=F32)
    proj_ref[...] = proj
    pos = jnp.full((1, 1), float(PAST_LEN), F32)
    cos, sin_signed, even = _rotation_tables(pos * jnp.ones((1, LANES), F32), inv_ref[...])
    even_full = jnp.broadcast_to(even, (x_ref.shape[0], LANES))
    for hh in range(RET_HEADS):
        ql = slice(OFF_RQ + hh * 128, OFF_RQ + (hh + 1) * 128)
        kl = slice(OFF_RK + hh * 128, OFF_RK + (hh + 1) * 128)
        proj_ref[:, ql] = _rotate_pairs(proj[:, ql], cos, sin_signed, even_full)
        proj_ref[:, kl] = _rotate_pairs(proj[:, kl], cos, sin_signed, even_full) * (RET_DK ** -0.5)


def _sample_proj(x, g1, win, inv):
    n = x.shape[0]
    return pl.pallas_call(
        _sample_proj_kernel,
        out_shape=jax.ShapeDtypeStruct((n, D_IN), F32),
        compiler_params=pltpu.CompilerParams(vmem_limit_bytes=VMEM_LIMIT_BYTES),
        name="sample_proj",
    )(x, g1, win, inv)


def _sample_state_kernel(proj_ref, st_ref, ck_ref, cv_ref, sinks_ref,
                         stn_ref, kn_ref, vn_ref, o2_ref, sres_ref, *, group):
    row8 = lax.broadcasted_iota(jnp.int32, (8, LANES), 0)
    lane8 = lax.broadcasted_iota(jnp.int32, (8, LANES), 1)
    lo1 = lax.broadcasted_iota(jnp.int32, (1, LANES), 1) < 64
    sink_col = sinks_ref[...]

    def sample_body(b, carry):
        row = pl.ds(b, 1)
        for hh in range(RET_HEADS):
            gamma = math.exp(LOG_DECAY[hh])
            q = proj_ref[row, OFF_RQ + hh * 128:OFF_RQ + (hh + 1) * 128]
            k = proj_ref[row, OFF_RK + hh * 128:OFF_RK + (hh + 1) * 128]
            v = proj_ref[row, OFF_RV + hh * 128:OFF_RV + (hh + 1) * 128]
            state = st_ref[b, hh]
            q8 = jnp.broadcast_to(q, (8, LANES)).astype(BF16)
            o2 = jnp.dot(q8, state.astype(BF16), preferred_element_type=F32)
            o2_ref[row, hh * 128:(hh + 1) * 128] = o2[0:1, :]
            k8 = jnp.where(row8 == 0, jnp.broadcast_to(k, (8, LANES)), 0.0).astype(BF16)
            v8 = jnp.broadcast_to(v, (8, LANES)).astype(BF16)
            outer = lax.dot_general(k8, v8, TN_DIMS, preferred_element_type=F32)
            stn_ref[b, hh] = state * gamma + outer
        kn_ref[b, 0:WINDOW - 1, :] = ck_ref[b, 1:WINDOW, :]
        vn_ref[b, 0:WINDOW - 1, :] = cv_ref[b, 1:WINDOW, :]
        kn_ref[b, WINDOW - 1:WINDOW, :] = proj_ref[row, OFF_SK:OFF_SK + 128]
        vn_ref[b, WINDOW - 1:WINDOW, :] = proj_ref[row, OFF_SV:OFF_SV + 128]
        kwin = kn_ref[b].astype(BF16)
        vwin = vn_ref[b].astype(BF16)
        q_pad = jnp.zeros((8, LANES), F32)
        for qb_idx in range(SWA_Q_HEADS // 2):
            g = qb_idx // 2
            blk = proj_ref[row, OFF_SQ + qb_idx * 128:OFF_SQ + (qb_idx + 1) * 128]
            rolled = pltpu.roll(blk, 64, 1)
            zero = jnp.zeros_like(blk)
            if g == 0:
                first, second = jnp.where(lo1, blk, zero), jnp.where(lo1, rolled, zero)
            else:
                first, second = jnp.where(lo1, zero, rolled), jnp.where(lo1, zero, blk)
            q_pad = jnp.where(row8 == 2 * qb_idx, jnp.broadcast_to(first, (8, LANES)), q_pad)
            q_pad = jnp.where(row8 == 2 * qb_idx + 1, jnp.broadcast_to(second, (8, LANES)), q_pad)
        sc = lax.dot_general(q_pad.astype(BF16), kwin, NT_DIMS,
                             preferred_element_type=F32) * (SWA_HD ** -0.5)
        m = jnp.maximum(jnp.max(sc, axis=-1, keepdims=True), sink_col)
        p = jnp.exp(sc - m)
        den = jnp.sum(p, axis=-1, keepdims=True) + jnp.exp(sink_col - m)
        sres_ref[b] = jnp.dot(p.astype(BF16), vwin, preferred_element_type=F32) / den
        return carry

    del lane8
    for b in range(group):
        sample_body(b, 0)


def _sample_state(proj, state, ck, cv, sinks, *, group):
    n = proj.shape[0]
    return pl.pallas_call(
        functools.partial(_sample_state_kernel, group=group),
        grid=(n // group,),
        in_specs=[
            pl.BlockSpec((group, D_IN), lambda i: (i, 0)),
            pl.BlockSpec((group, RET_HEADS, 128, 128), lambda i: (i, 0, 0, 0)),
            pl.BlockSpec((group, WINDOW, 128), lambda i: (i, 0, 0)),
            pl.BlockSpec((group, WINDOW, 128), lambda i: (i, 0, 0)),
            pl.BlockSpec((SWA_Q_HEADS, 1), lambda i: (0, 0)),
        ],
        out_specs=[
            pl.BlockSpec((group, RET_HEADS, 128, 128), lambda i: (i, 0, 0, 0)),
            pl.BlockSpec((group, WINDOW, 128), lambda i: (i, 0, 0)),
            pl.BlockSpec((group, WINDOW, 128), lambda i: (i, 0, 0)),
            pl.BlockSpec((group, 512), lambda i: (i, 0)),
            pl.BlockSpec((group, SWA_Q_HEADS, 128), lambda i: (i, 0, 0)),
        ],
        out_shape=[
            jax.ShapeDtypeStruct((n, RET_HEADS, 128, 128), F32),
            jax.ShapeDtypeStruct((n, WINDOW, 128), F32),
            jax.ShapeDtypeStruct((n, WINDOW, 128), F32),
            jax.ShapeDtypeStruct((n, 512), F32),
            jax.ShapeDtypeStruct((n, SWA_Q_HEADS, 128), F32),
        ],
        compiler_params=pltpu.CompilerParams(
            dimension_semantics=("arbitrary",), vmem_limit_bytes=VMEM_LIMIT_BYTES),
        name="sample_state",
    )(proj, state, ck, cv, sinks)


def _sample_finish_kernel(x_ref, proj_ref, o2_ref, sres_ref, wout_ref, g2_ref,
                          h_ref, h2t_ref, mix_ref):
    n = x_ref.shape[0]
    for hh in range(RET_HEADS):
        gamma = math.exp(LOG_DECAY[hh])
        lanes = slice(hh * 128, (hh + 1) * 128)
        q = proj_ref[:, OFF_RQ + hh * 128:OFF_RQ + (hh + 1) * 128]
        k = proj_ref[:, OFF_RK + hh * 128:OFF_RK + (hh + 1) * 128]
        v = proj_ref[:, OFF_RV + hh * 128:OFF_RV + (hh + 1) * 128]
        gate = proj_ref[:, OFF_RG + hh * 128:OFF_RG + (hh + 1) * 128]
        qk = jnp.sum(q * k, axis=-1, keepdims=True)
        o = qk * v + o2_ref[:, lanes] * gamma
        o = o * lax.rsqrt(jnp.mean(o * o, axis=-1, keepdims=True) + RMS_EPS)
        mix_ref[:, lanes] = (o * _silu(gate)).astype(BF16)
    lo = lax.broadcasted_iota(jnp.int32, (n, LANES), 1) < 64
    for qb_idx in range(SWA_Q_HEADS // 2):
        r0 = sres_ref[:, (2 * qb_idx) * 128:(2 * qb_idx + 1) * 128]
        r1 = sres_ref[:, (2 * qb_idx + 1) * 128:(2 * qb_idx + 2) * 128]
        if qb_idx // 2 == 0:
            blk = jnp.where(lo, r0, pltpu.roll(r1, 64, 1))
        else:
            blk = jnp.where(lo, pltpu.roll(r0, 64, 1), r1)
        mix_ref[:, 512 + qb_idx * 128:512 + (qb_idx + 1) * 128] = blk.astype(BF16)
    h = x_ref[...] + jnp.dot(mix_ref[...], wout_ref[...], preferred_element_type=F32)
    h_ref[...] = h
    h2t_ref[...] = _rms_norm(h, g2_ref[...]).T.astype(BF16)


def _sample_finish(x, proj, o2, sres, wout, g2):
    n, d = x.shape
    return pl.pallas_call(
        _sample_finish_kernel,
        out_shape=[jax.ShapeDtypeStruct((n, d), F32), jax.ShapeDtypeStruct((d, n), BF16)],
        scratch_shapes=[pltpu.VMEM((n, d), BF16)],
        compiler_params=pltpu.CompilerParams(vmem_limit_bytes=VMEM_LIMIT_BYTES),
        name="sample_finish",
    )(x, proj, o2, sres, wout, g2)


def _top16_desc(s):
    work = s
    rows = []
    for p in range(PEER_TOPK):
        m = jnp.max(work, axis=0, keepdims=True)
        rows.append(m)
        if p + 1 < PEER_TOPK:
            work = jnp.where(work == m, NEG_INF, work)
    return rows


def _stack8(rows, row8):
    out = jnp.zeros(row8.shape, F32)
    for r, v in enumerate(rows):
        out = jnp.where(row8 == r, jnp.broadcast_to(v, row8.shape), out)
    return out


def _routing_tables(s0, s1):
    a = _top16_desc(s0)
    b = _top16_desc(s1)
    row8 = lax.broadcasted_iota(jnp.int32, (8, s0.shape[1]), 0)
    a_lo, a_hi = _stack8(a[:8], row8), _stack8(a[8:], row8)
    b_lo, b_hi = _stack8(b[:8], row8), _stack8(b[8:], row8)
    neg = jnp.full(row8.shape, NEG_INF, F32)
    cands = [
        a[0] + b_lo, a[0] + b_hi, a[1] + b_lo,
        jnp.where(row8 < 5, a[2] + b_lo, neg),
        jnp.where(row8 < 4, a[3] + b_lo, neg),
        b[0] + a_hi,
        jnp.where(row8 >= 4, b[0] + a_lo, neg),
        jnp.where(row8 >= 4, b[1] + a_lo, neg),
        jnp.where(row8 == 4, b[2] + a_lo, neg),
    ]
    work = cands
    tau = None
    for it in range(PEER_TOPK):
        m = functools.reduce(jnp.maximum, work)
        tau = jnp.max(m, axis=0, keepdims=True)
        if it + 1 < PEER_TOPK:
            work = [jnp.where(w == tau, NEG_INF, w) for w in work]
    top = a[0] + b[0]
    z = functools.reduce(
        lambda x, y: x + y,
        [jnp.sum(jnp.where(c >= tau, jnp.exp(c - top), 0.0), axis=0, keepdims=True) for c in cands])
    n0 = jnp.zeros(s0.shape, F32)
    rank1 = jnp.zeros(s1.shape, F32)
    for q in range(PEER_TOPK):
        n0 = n0 + jnp.where(s0 + b[q] >= tau, 1.0, 0.0)
        rank1 = rank1 + jnp.where(s1 < b[q], 1.0, 0.0)
    e1 = jnp.exp(s1 - b[0])
    e0n = jnp.exp(s0 - a[0]) * (1.0 / z)
    return rank1, e1, n0, e0n


def _routing_kernel(h2t_ref, wqt_ref, keys_ref, r1_ref, e1_ref, n0_ref, e0_ref,
                    q_ref, s_ref, *, tt):
    q_ref[...] = jnp.dot(wqt_ref[...], h2t_ref[...], preferred_element_type=F32)
    n_lg = tt // LANES

    def head_body(h, carry):
        for c in range(2):
            r0 = pl.multiple_of((2 * h + c) * 128, 128)
            q_hc = q_ref[pl.ds(r0, 128), :].astype(BF16)
            s_ref[c] = jnp.dot(keys_ref[2 * h + c], q_hc, preferred_element_type=F32)
        out_rows = pl.ds(pl.multiple_of(h * 128, 128), 128)
        for lg in range(n_lg):
            lanes = slice(lg * LANES, (lg + 1) * LANES)
            rank1, e1, n0, e0n = _routing_tables(s_ref[0, :, lanes], s_ref[1, :, lanes])
            r1_ref[lg, out_rows, :] = rank1.astype(BF16)
            e1_ref[lg, out_rows, :] = e1.astype(BF16)
            n0_ref[lg, out_rows, :] = n0
            e0_ref[lg, out_rows, :] = e0n
        return carry

    lax.fori_loop(0, PEER_HEADS, head_body, 0)


def _routing(h2t, wqt, keys, *, tt):
    d, t = h2t.shape
    rows = PEER_HEADS * PEER_N_KEYS
    tab = lambda dt: jax.ShapeDtypeStruct((t // LANES, rows, LANES), dt)
    tab_spec = pl.BlockSpec((tt // LANES, rows, LANES), lambda i: (i, 0, 0))
    return pl.pallas_call(
        functools.partial(_routing_kernel, tt=tt),
        grid=(t // tt,),
        in_specs=[
            pl.BlockSpec((d, tt), lambda i: (0, i)),
            pl.BlockSpec(wqt.shape, lambda i: (0, 0)),
            pl.BlockSpec(keys.shape, lambda i: (0, 0, 0)),
        ],
        out_specs=[tab_spec, tab_spec, tab_spec, tab_spec],
        out_shape=[tab(BF16), tab(BF16), tab(F32), tab(F32)],
        scratch_shapes=[pltpu.VMEM((wqt.shape[0], tt), F32), pltpu.VMEM((2, 128, tt), F32)],
        compiler_params=pltpu.CompilerParams(
            dimension_semantics=("arbitrary",), vmem_limit_bytes=VMEM_LIMIT_BYTES),
        name="peer_routing",
    )(h2t, wqt, keys)


def _experts_kernel(xt_ref, u_ref, vt_ref, r1_ref, e1_ref, n0_ref, e0_ref, h_ref, fg_ref,
                    y_ref, acc_ref, hd_ref, at_ref, *, tt, et):
    e = pl.program_id(1)
    n_ib = et // PEER_N_KEYS
    sub = 16

    @pl.when(e == 0)
    def _init():
        acc_ref[...] = jnp.zeros_like(acc_ref)

    hd_ref[...] = jnp.dot(u_ref[...], xt_ref[...], preferred_element_type=F32)

    for lg in range(tt // LANES):
        lanes = slice(lg * LANES, (lg + 1) * LANES)
        for ii in range(n_ib):
            rows = slice(ii * 128, (ii + 1) * 128)
            w = jnp.zeros((128 // sub, sub, LANES), BF16)
            for h in range(PEER_HEADS):
                grp = pl.ds(pl.multiple_of(h * 128 + e * n_ib + (ii // 8) * 8, 8), 8)
                n0b = jnp.broadcast_to(n0_ref[lg, grp, :][ii % 8:ii % 8 + 1, :], (sub, LANES))
                e0b = jnp.broadcast_to(e0_ref[lg, grp, :][ii % 8:ii % 8 + 1, :], (sub, LANES))
                n0b, e0b = n0b.astype(BF16), e0b.astype(BF16)
                r1 = r1_ref[lg, h * 128:(h + 1) * 128, :].reshape(128 // sub, sub, LANES)
                e1 = e1_ref[lg, h * 128:(h + 1) * 128, :].reshape(128 // sub, sub, LANES)
                w = w + jnp.where(r1 < n0b[None], e1 * e0b[None], jnp.zeros_like(e1))
            x = hd_ref[rows, lanes]
            gelu = 0.5 * x * (1.0 + lax.erf(x * math.sqrt(0.5)))
            at_ref[rows, lanes] = gelu.astype(BF16) * w.reshape(128, LANES)
    acc_ref[...] += jnp.dot(vt_ref[...], at_ref[...], preferred_element_type=F32)

    @pl.when(e == pl.num_programs(1) - 1)
    def _finish():
        y_ref[...] = _rms_norm(h_ref[...] + acc_ref[...].T, fg_ref[...])


def _experts(xt, u, vt, r1, e1, n0, e0, h, fg, *, tt, et):
    d, t = xt.shape
    n_exp = u.shape[0]
    rows = r1.shape[1]
    assert et % (8 * PEER_N_KEYS) == 0 and t % tt == 0 and n_exp % et == 0
    tab_spec = pl.BlockSpec((tt // LANES, rows, LANES), lambda i, e: (i, 0, 0))
    return pl.pallas_call(
        functools.partial(_experts_kernel, tt=tt, et=et),
        grid=(t // tt, n_exp // et),
        in_specs=[
            pl.BlockSpec((d, tt), lambda i, e: (0, i)),
            pl.BlockSpec((et, d), lambda i, e: (e, 0)),
            pl.BlockSpec((d, et), lambda i, e: (0, e)),
            tab_spec, tab_spec, tab_spec, tab_spec,
            pl.BlockSpec((tt, d), lambda i, e: (i, 0)),
            pl.BlockSpec((1, d), lambda i, e: (0, 0)),
        ],
        out_specs=pl.BlockSpec((tt, d), lambda i, e: (i, 0)),
        out_shape=jax.ShapeDtypeStruct((t, d), F32),
        scratch_shapes=[
            pltpu.VMEM((d, tt), F32),
            pltpu.VMEM((et, tt), F32),
            pltpu.VMEM((et, tt), BF16),
        ],
        compiler_params=pltpu.CompilerParams(
            dimension_semantics=("arbitrary", "arbitrary"),
            vmem_limit_bytes=VMEM_LIMIT_BYTES),
        name="peer_experts",
    )(xt, u, vt, r1, e1, n0, e0, h, fg)


def _peer_and_final(h, h2t, wqt, keys, u, vt, fg, *, tt_route, tt, et):
    r1, e1, n0, e0 = _routing(h2t, wqt, keys, tt=tt_route)
    return _experts(h2t, u, vt, r1, e1, n0, e0, h, fg, tt=tt, et=et)


def kernel(x_prompt, x_sample, state_ret, cache_swa_k, cache_swa_v, norm1_g, w_in, swa_sinks, w_out, norm2_g, peer_w_q, peer_sub_keys, peer_u, peer_v, final_g):
    batch, seq, d = x_prompt.shape
    n_s = x_sample.shape[0]
    g1 = norm1_g.reshape(1, d)
    g2 = norm2_g.reshape(1, d)
    fg = final_g.reshape(1, d)
    win = w_in.astype(BF16)
    wout = w_out.astype(BF16)
    sinks = swa_sinks.astype(F32).reshape(SWA_Q_HEADS, 1)
    wqt = peer_w_q.T.astype(BF16)
    keys = peer_sub_keys.reshape(PEER_HEADS * 2, PEER_N_KEYS, -1).astype(BF16)
    u = peer_u.astype(BF16)
    vt = peer_v.T.astype(BF16)
    half = RET_DK // 2
    inv = 1.0 / (ROPE_BASE ** jnp.linspace(0.0, 1.0, half, dtype=F32))
    inv = jnp.repeat(inv, 2).reshape(1, RET_DK)

    h_p, h2t_p, st_p, k_p, v_p = _prompt_mixer(x_prompt, g1, win, sinks, wout, g2, inv, tq=512)
    y_p = _peer_and_final(h_p, h2t_p, wqt, keys, u, vt, fg, tt_route=256, tt=512, et=1024)

    xs = x_sample.reshape(n_s, d)
    proj = _sample_proj(xs, g1, win, inv)
    st_s, k_s, v_s, o2, sres = _sample_state(
        proj, state_ret, cache_swa_k.reshape(n_s, WINDOW, 128),
        cache_swa_v.reshape(n_s, WINDOW, 128), sinks, group=8)
    h_s, h2t_s = _sample_finish(xs, proj, o2, sres.reshape(n_s, SWA_Q_HEADS * 128), wout, g2)
    y_s = _peer_and_final(h_s, h2t_s, wqt, keys, u, vt, fg, tt_route=128, tt=128, et=1024)

    kv_shape = (WINDOW, 2, SWA_HD)
    return (y_p.reshape(batch, seq, d), y_s.reshape(n_s, 1, d), st_p,
            k_p.reshape(batch, *kv_shape), v_p.reshape(batch, *kv_shape),
            st_s, k_s.reshape(n_s, *kv_shape), v_s.reshape(n_s, *kv_shape))
```

```python
import functools
import math

import jax
import jax.numpy as jnp
import numpy as np
from jax import lax
from jax.experimental import pallas as pl
from jax.experimental.pallas import tpu as pltpu

F32 = jnp.float32
BF16 = jnp.bfloat16

D_MODEL = 1024
SEQ = 8192
PAST_LEN = 8192
RET_HEADS = 4
RET_DK = 128
CHUNK = 128
ROPE_BASE = 10000.0
SWA_Q_HEADS = 8
SWA_HD = 64
WINDOW = 128
D_IN = 2816
OFF_RQ, OFF_RK, OFF_RV, OFF_RG, OFF_SQ, OFF_SK, OFF_SV = 0, 512, 1024, 1536, 2048, 2560, 2688
PEER_HEADS = 8
PEER_N_KEYS = 128
PEER_N_EXPERTS = PEER_N_KEYS * PEER_N_KEYS
PEER_TOPK = 16
RMS_EPS = 1e-6
LANES = 128

LOG_DECAY = tuple(math.log(1.0 - 2.0 ** (-5.0 - h)) for h in range(RET_HEADS))

VMEM_LIMIT_BYTES = 56 * 1024 * 1024

NEG_INF = float("-inf")
NT_DIMS = (((1,), (1,)), ((), ()))
TN_DIMS = (((0,), (0,)), ((), ()))


def _rms_norm(x, g):
    return x * lax.rsqrt(jnp.mean(x * x, axis=-1, keepdims=True) + RMS_EPS) * g


def _rotation_tables(pos, inv):
    ang = pos * inv
    cos = jnp.cos(ang)
    sin = jnp.sin(ang)
    lane = lax.broadcasted_iota(jnp.int32, ang.shape, 1)
    even = (lane & 1) == 0
    return cos, jnp.where(even, -sin, sin), even


def _rotate_pairs(x, cos, sin_signed, even):
    partner = jnp.where(even, pltpu.roll(x, LANES - 1, 1), pltpu.roll(x, 1, 1))
    return x * cos + partner * sin_signed


def _silu(g):
    return g * jax.nn.sigmoid(g)


def _head_half_variants(a, lo):
    rolled = pltpu.roll(a, 64, 1)
    zero = jnp.zeros_like(a)
    return (jnp.where(lo, a, zero), jnp.where(lo, zero, rolled),
            jnp.where(lo, rolled, zero), jnp.where(lo, zero, a))


def _prompt_mixer_kernel(x_ref, g1_ref, win_ref, sinks_ref, wout_ref, g2_ref, inv_ref,
                         h_ref, h2t_ref, st_ref, kp_ref, vp_ref,
                         proj_ref, state_ref, kcat_ref, vcat_ref, mix_ref,
                         dmask_ref, qdec_ref, kdec_ref, *, tq):
    s = pl.program_id(1)
    n_chunks = tq // CHUNK

    @pl.when(s == 0)
    def _init():
        state_ref[...] = jnp.zeros_like(state_ref)
        kcat_ref[...] = jnp.zeros_like(kcat_ref)
        vcat_ref[...] = jnp.zeros_like(vcat_ref)
        t_row = lax.broadcasted_iota(jnp.int32, (CHUNK, CHUNK), 0).astype(F32)
        t_col = lax.broadcasted_iota(jnp.int32, (CHUNK, CHUNK), 1).astype(F32)
        diff = t_row - t_col
        causal = diff >= 0
        for hh in range(RET_HEADS):
            lg = LOG_DECAY[hh]
            dmask_ref[hh] = jnp.where(causal, jnp.exp(jnp.where(causal, diff, 0.0) * lg), 0.0)
            qdec_ref[hh] = jnp.exp((t_row + 1.0) * lg)
            kdec_ref[hh] = jnp.exp((CHUNK - 1.0 - t_row) * lg)

    x = x_ref[...]
    xn = _rms_norm(x, g1_ref[...]).astype(BF16)
    proj_ref[...] = jnp.dot(xn, win_ref[...], preferred_element_type=F32)

    def chunk_body(c, carry):
        r0 = pl.multiple_of(c * CHUNK, CHUNK)
        rows = pl.ds(r0, CHUNK)
        chunk_idx = s * n_chunks + c
        t_local = lax.broadcasted_iota(jnp.int32, (CHUNK, 1), 0)
        pos = (chunk_idx * CHUNK + t_local).astype(F32)
        cos, sin_signed, even = _rotation_tables(pos, inv_ref[...])

        for hh in range(RET_HEADS):
            lanes = slice(hh * RET_DK, (hh + 1) * RET_DK)
            q = _rotate_pairs(proj_ref[rows, OFF_RQ + hh * 128:OFF_RQ + (hh + 1) * 128],
                              cos, sin_signed, even)
            k = _rotate_pairs(proj_ref[rows, OFF_RK + hh * 128:OFF_RK + (hh + 1) * 128],
                              cos, sin_signed, even) * (RET_DK ** -0.5)
            v = proj_ref[rows, OFF_RV + hh * 128:OFF_RV + (hh + 1) * 128]
            gate = proj_ref[rows, OFF_RG + hh * 128:OFF_RG + (hh + 1) * 128]
            qb, vb = q.astype(BF16), v.astype(BF16)
            scores = lax.dot_general(qb, k.astype(BF16), NT_DIMS,
                                     preferred_element_type=F32) * dmask_ref[hh]
            state = state_ref[hh]
            o = (jnp.dot(scores.astype(BF16), vb, preferred_element_type=F32)
                 + jnp.dot(qb, state.astype(BF16), preferred_element_type=F32) * qdec_ref[hh])
            k_dec = (k * kdec_ref[hh]).astype(BF16)
            state_ref[hh] = (state * math.exp(CHUNK * LOG_DECAY[hh])
                             + lax.dot_general(k_dec, vb, TN_DIMS, preferred_element_type=F32))
            o = o * lax.rsqrt(jnp.mean(o * o, axis=-1, keepdims=True) + RMS_EPS)
            mix_ref[rows, lanes] = (o * _silu(gate)).astype(BF16)

        kc = proj_ref[rows, OFF_SK:OFF_SK + 128]
        vc = proj_ref[rows, OFF_SV:OFF_SV + 128]
        kp_ref[...] = kc
        vp_ref[...] = vc
        lane = lax.broadcasted_iota(jnp.int32, (CHUNK, LANES), 1)
        lo = lane < 64
        for idx, (kvar, vvar) in enumerate(zip(_head_half_variants(kc, lo),
                                                _head_half_variants(vc, lo))):
            kcat_ref[idx, CHUNK:2 * CHUNK, :] = kvar.astype(BF16)
            vcat_ref[idx, CHUNK:2 * CHUNK, :] = vvar.astype(BF16)
        q_row = lax.broadcasted_iota(jnp.int32, (CHUNK, 2 * CHUNK), 0)
        k_col = lax.broadcasted_iota(jnp.int32, (CHUNK, 2 * CHUNK), 1)
        valid = (k_col >= q_row + 1) & (k_col <= q_row + WINDOW)
        valid = valid & ((k_col >= CHUNK) | (chunk_idx > 0))
        for qb_idx in range(SWA_Q_HEADS // 2):
            g = qb_idx // 2
            q_blk = proj_ref[rows, OFF_SQ + qb_idx * 128:OFF_SQ + (qb_idx + 1) * 128].astype(BF16)
            acc = jnp.zeros((CHUNK, LANES), F32)
            for par in range(2):
                head = 2 * qb_idx + par
                sink = sinks_ref[head:head + 1, :]
                sc = lax.dot_general(q_blk, kcat_ref[g * 2 + par], NT_DIMS,
                                     preferred_element_type=F32) * (SWA_HD ** -0.5)
                sc = jnp.where(valid, sc, NEG_INF)
                m = jnp.maximum(jnp.max(sc, axis=-1, keepdims=True), sink)
                p = jnp.exp(sc - m)
                den = jnp.sum(p, axis=-1, keepdims=True) + jnp.exp(sink - m)
                acc = acc + jnp.dot(p.astype(BF16), vcat_ref[g * 2 + par],
                                    preferred_element_type=F32) / den
            mix_ref[rows, 512 + qb_idx * 128:512 + (qb_idx + 1) * 128] = acc.astype(BF16)
        for idx in range(4):
            kcat_ref[idx, 0:CHUNK, :] = kcat_ref[idx, CHUNK:2 * CHUNK, :]
            vcat_ref[idx, 0:CHUNK, :] = vcat_ref[idx, CHUNK:2 * CHUNK, :]
        return carry

    lax.fori_loop(0, n_chunks, chunk_body, 0)

    h = x + jnp.dot(mix_ref[...], wout_ref[...], preferred_element_type=F32)
    h_ref[...] = h
    h2t_ref[...] = pltpu.bitcast(_rms_norm(h, g2_ref[...]).T.astype(BF16), jnp.uint32)
    st_ref[...] = state_ref[...]


def _prompt_mixer(x, g1, win, sinks, wout, g2, inv, *, tq):
    batch, seq, d = x.shape
    ns = seq // tq
    full = lambda shape: pl.BlockSpec(shape, lambda b, s: (0,) * len(shape))
    return pl.pallas_call(
        functools.partial(_prompt_mixer_kernel, tq=tq),
        grid=(batch, ns),
        in_specs=[
            pl.BlockSpec((None, tq, d), lambda b, s: (b, s, 0)),
            full((1, d)), full((d, D_IN)), full((SWA_Q_HEADS, 1)), full((d, d)), full((1, d)),
            full((1, LANES)),
        ],
        out_specs=[
            pl.BlockSpec((tq, d), lambda b, s: (b * ns + s, 0)),
            pl.BlockSpec((d // 2, tq), lambda b, s: (0, b * ns + s)),
            pl.BlockSpec((None, RET_HEADS, 128, 128), lambda b, s: (b, 0, 0, 0)),
            pl.BlockSpec((None, WINDOW, 128), lambda b, s: (b, 0, 0)),
            pl.BlockSpec((None, WINDOW, 128), lambda b, s: (b, 0, 0)),
        ],
        out_shape=[
            jax.ShapeDtypeStruct((batch * seq, d), F32),
            jax.ShapeDtypeStruct((d // 2, batch * seq), jnp.uint32),
            jax.ShapeDtypeStruct((batch, RET_HEADS, 128, 128), F32),
            jax.ShapeDtypeStruct((batch, WINDOW, 128), F32),
            jax.ShapeDtypeStruct((batch, WINDOW, 128), F32),
        ],
        scratch_shapes=[
            pltpu.VMEM((tq, D_IN), F32),
            pltpu.VMEM((RET_HEADS, 128, 128), F32),
            pltpu.VMEM((4, 2 * CHUNK, LANES), BF16),
            pltpu.VMEM((4, 2 * CHUNK, LANES), BF16),
            pltpu.VMEM((tq, d), BF16),
            pltpu.VMEM((RET_HEADS, CHUNK, CHUNK), F32),
            pltpu.VMEM((RET_HEADS, CHUNK, CHUNK), F32),
            pltpu.VMEM((RET_HEADS, CHUNK, CHUNK), F32),
        ],
        compiler_params=pltpu.CompilerParams(
            dimension_semantics=("arbitrary", "arbitrary"),
            vmem_limit_bytes=VMEM_LIMIT_BYTES),
        name="prompt_mixer",
    )(x, g1, win, sinks, wout, g2, inv)


def _sample_proj_kernel(x_ref, g1_ref, win_ref, inv_ref, proj_ref):
    xn = _rms_norm(x_ref[...], g1_ref[...]).astype(BF16)
    proj = jnp.dot(xn, win_ref[...], preferred_element_type=F32)
    proj_ref[...] = proj
    pos = jnp.full((1, 1), float(PAST_LEN), F32)
    cos, sin_signed, even = _rotation_tables(pos * jnp.ones((1, LANES), F32), inv_ref[...])
    even_full = jnp.broadcast_to(even, (x_ref.shape[0], LANES))
    for hh in range(RET_HEADS):
        ql = slice(OFF_RQ + hh * 128, OFF_RQ + (hh + 1) * 128)
        kl = slice(OFF_RK + hh * 128, OFF_RK + (hh + 1) * 128)
        proj_ref[:, ql] = _rotate_pairs(proj[:, ql], cos, sin_signed, even_full)
        proj_ref[:, kl] = _rotate_pairs(proj[:, kl], cos, sin_signed, even_full) * (RET_DK ** -0.5)


def _sample_proj(x, g1, win, inv):
    n = x.shape[0]
    return pl.pallas_call(
        _sample_proj_kernel,
        out_shape=jax.ShapeDtypeStruct((n, D_IN), F32),
        compiler_params=pltpu.CompilerParams(vmem_limit_bytes=VMEM_LIMIT_BYTES),
        name="sample_proj",
    )(x, g1, win, inv)


def _sample_state_kernel(proj_ref, st_ref, ck_ref, cv_ref, sinks_ref,
                         stn_ref, kn_ref, vn_ref, o2_ref, sres_ref, *, group):
    row8 = lax.broadcasted_iota(jnp.int32, (8, LANES), 0)
    lane8 = lax.broadcasted_iota(jnp.int32, (8, LANES), 1)
    lo1 = lax.broadcasted_iota(jnp.int32, (1, LANES), 1) < 64
    sink_col = sinks_ref[...]

    def sample_body(b, carry):
        row = pl.ds(b, 1)
        for hh in range(RET_HEADS):
            gamma = math.exp(LOG_DECAY[hh])
            q = proj_ref[row, OFF_RQ + hh * 128:OFF_RQ + (hh + 1) * 128]
            k = proj_ref[row, OFF_RK + hh * 128:OFF_RK + (hh + 1) * 128]
            v = proj_ref[row, OFF_RV + hh * 128:OFF_RV + (hh + 1) * 128]
            state = st_ref[b, hh]
            q8 = jnp.broadcast_to(q, (8, LANES)).astype(BF16)
            o2 = jnp.dot(q8, state.astype(BF16), preferred_element_type=F32)
            o2_ref[row, hh * 128:(hh + 1) * 128] = o2[0:1, :]
            k8 = jnp.where(row8 == 0, jnp.broadcast_to(k, (8, LANES)), 0.0).astype(BF16)
            v8 = jnp.broadcast_to(v, (8, LANES)).astype(BF16)
            outer = lax.dot_general(k8, v8, TN_DIMS, preferred_element_type=F32)
            stn_ref[b, hh] = state * gamma + outer
        kn_ref[b, 0:WINDOW - 1, :] = ck_ref[b, 1:WINDOW, :]
        vn_ref[b, 0:WINDOW - 1, :] = cv_ref[b, 1:WINDOW, :]
        kn_ref[b, WINDOW - 1:WINDOW, :] = proj_ref[row, OFF_SK:OFF_SK + 128]
        vn_ref[b, WINDOW - 1:WINDOW, :] = proj_ref[row, OFF_SV:OFF_SV + 128]
        kwin = kn_ref[b].astype(BF16)
        vwin = vn_ref[b].astype(BF16)
        q_pad = jnp.zeros((8, LANES), F32)
        for qb_idx in range(SWA_Q_HEADS // 2):
            g = qb_idx // 2
            blk = proj_ref[row, OFF_SQ + qb_idx * 128:OFF_SQ + (qb_idx + 1) * 128]
            rolled = pltpu.roll(blk, 64, 1)
            zero = jnp.zeros_like(blk)
            if g == 0:
                first, second = jnp.where(lo1, blk, zero), jnp.where(lo1, rolled, zero)
            else:
                first, second = jnp.where(lo1, zero, rolled), jnp.where(lo1, zero, blk)
            q_pad = jnp.where(row8 == 2 * qb_idx, jnp.broadcast_to(first, (8, LANES)), q_pad)
            q_pad = jnp.where(row8 == 2 * qb_idx + 1, jnp.broadcast_to(second, (8, LANES)), q_pad)
        sc = lax.dot_general(q_pad.astype(BF16), kwin, NT_DIMS,
                             preferred_element_type=F32) * (SWA_HD ** -0.5)
        m = jnp.maximum(jnp.max(sc, axis=-1, keepdims=True), sink_col)
        p = jnp.exp(sc - m)
        den = jnp.sum(p, axis=-1, keepdims=True) + jnp.exp(sink_col - m)
        sres_ref[b] = jnp.dot(p.astype(BF16), vwin, preferred_element_type=F32) / den
        return carry

    del lane8
    for b in range(group):
        sample_body(b, 0)


def _sample_state(proj, state, ck, cv, sinks, *, group):
    n = proj.shape[0]
    return pl.pallas_call(
        functools.partial(_sample_state_kernel, group=group),
        grid=(n // group,),
        in_specs=[
            pl.BlockSpec((group, D_IN), lambda i: (i, 0)),
            pl.BlockSpec((group, RET_HEADS, 128, 128), lambda i: (i, 0, 0, 0)),
            pl.BlockSpec((group, WINDOW, 128), lambda i: (i, 0, 0)),
            pl.BlockSpec((group, WINDOW, 128), lambda i: (i, 0, 0)),
            pl.BlockSpec((SWA_Q_HEADS, 1), lambda i: (0, 0)),
        ],
        out_specs=[
            pl.BlockSpec((group, RET_HEADS, 128, 128), lambda i: (i, 0, 0, 0)),
            pl.BlockSpec((group, WINDOW, 128), lambda i: (i, 0, 0)),
            pl.BlockSpec((group, WINDOW, 128), lambda i: (i, 0, 0)),
            pl.BlockSpec((group, 512), lambda i: (i, 0)),
            pl.BlockSpec((group, SWA_Q_HEADS, 128), lambda i: (i, 0, 0)),
        ],
        out_shape=[
            jax.ShapeDtypeStruct((n, RET_HEADS, 128, 128), F32),
            jax.ShapeDtypeStruct((n, WINDOW, 128), F32),
            jax.ShapeDtypeStruct((n, WINDOW, 128), F32),
            jax.ShapeDtypeStruct((n, 512), F32),
            jax.ShapeDtypeStruct((n, SWA_Q_HEADS, 128), F32),
        ],
        compiler_params=pltpu.CompilerParams(
            dimension_semantics=("arbitrary",), vmem_limit_bytes=VMEM_LIMIT_BYTES),
        name="sample_state",
    )(proj, state, ck, cv, sinks)


def _sample_finish_kernel(x_ref, proj_ref, o2_ref, sres_ref, wout_ref, g2_ref,
                          h_ref, h2t_ref, mix_ref):
    n = x_ref.shape[0]
    for hh in range(RET_HEADS):
        gamma = math.exp(LOG_DECAY[hh])
        lanes = slice(hh * 128, (hh + 1) * 128)
        q = proj_ref[:, OFF_RQ + hh * 128:OFF_RQ + (hh + 1) * 128]
        k = proj_ref[:, OFF_RK + hh * 128:OFF_RK + (hh + 1) * 128]
        v = proj_ref[:, OFF_RV + hh * 128:OFF_RV + (hh + 1) * 128]
        gate = proj_ref[:, OFF_RG + hh * 128:OFF_RG + (hh + 1) * 128]
        qk = jnp.sum(q * k, axis=-1, keepdims=True)
        o = qk * v + o2_ref[:, lanes] * gamma
        o = o * lax.rsqrt(jnp.mean(o * o, axis=-1, keepdims=True) + RMS_EPS)
        mix_ref[:, lanes] = (o * _silu(gate)).astype(BF16)
    lo = lax.broadcasted_iota(jnp.int32, (n, LANES), 1) < 64
    for qb_idx in range(SWA_Q_HEADS // 2):
        r0 = sres_ref[:, (2 * qb_idx) * 128:(2 * qb_idx + 1) * 128]
        r1 = sres_ref[:, (2 * qb_idx + 1) * 128:(2 * qb_idx + 2) * 128]
        if qb_idx // 2 == 0:
            blk = jnp.where(lo, r0, pltpu.roll(r1, 64, 1))
        else:
            blk = jnp.where(lo, pltpu.roll(r0, 64, 1), r1)
        mix_ref[:, 512 + qb_idx * 128:512 + (qb_idx + 1) * 128] = blk.astype(BF16)
    h = x_ref[...] + jnp.dot(mix_ref[...], wout_ref[...], preferred_element_type=F32)
    h_ref[...] = h
    h2t_ref[...] = pltpu.bitcast(_rms_norm(h, g2_ref[...]).T.astype(BF16), jnp.uint32)


def _sample_finish(x, proj, o2, sres, wout, g2):
    n, d = x.shape
    return pl.pallas_call(
        _sample_finish_kernel,
        out_shape=[jax.ShapeDtypeStruct((n, d), F32),
                   jax.ShapeDtypeStruct((d // 2, n), jnp.uint32)],
        scratch_shapes=[pltpu.VMEM((n, d), BF16)],
        compiler_params=pltpu.CompilerParams(vmem_limit_bytes=VMEM_LIMIT_BYTES),
        name="sample_finish",
    )(x, proj, o2, sres, wout, g2)


def _top16_desc(s):
    work = s
    rows = []
    for p in range(PEER_TOPK):
        m = jnp.max(work, axis=0, keepdims=True)
        rows.append(m)
        if p + 1 < PEER_TOPK:
            work = jnp.where(work == m, NEG_INF, work)
    return rows


def _stack8(rows, row8):
    out = jnp.zeros(row8.shape, F32)
    for r, v in enumerate(rows):
        out = jnp.where(row8 == r, jnp.broadcast_to(v, row8.shape), out)
    return out


def _routing_tables(s0, s1):
    a = _top16_desc(s0)
    b = _top16_desc(s1)
    row8 = lax.broadcasted_iota(jnp.int32, (8, s0.shape[1]), 0)
    a_lo, a_hi = _stack8(a[:8], row8), _stack8(a[8:], row8)
    b_lo, b_hi = _stack8(b[:8], row8), _stack8(b[8:], row8)
    neg = jnp.full(row8.shape, NEG_INF, F32)
    cands = [
        a[0] + b_lo, a[0] + b_hi, a[1] + b_lo,
        jnp.where(row8 < 5, a[2] + b_lo, neg),
        jnp.where(row8 < 4, a[3] + b_lo, neg),
        b[0] + a_hi,
        jnp.where(row8 >= 4, b[0] + a_lo, neg),
        jnp.where(row8 >= 4, b[1] + a_lo, neg),
        jnp.where(row8 == 4, b[2] + a_lo, neg),
    ]
    work = cands
    tau = None
    for it in range(PEER_TOPK):
        m = functools.reduce(jnp.maximum, work)
        tau = jnp.max(m, axis=0, keepdims=True)
        if it + 1 < PEER_TOPK:
            work = [jnp.where(w == tau, NEG_INF, w) for w in work]
    top = a[0] + b[0]
    z = functools.reduce(
        lambda x, y: x + y,
        [jnp.sum(jnp.where(c >= tau, jnp.exp(c - top), 0.0), axis=0, keepdims=True) for c in cands])
    n0 = jnp.zeros(s0.shape, F32)
    rank1 = jnp.zeros(s1.shape, F32)
    for q in range(PEER_TOPK):
        n0 = n0 + jnp.where(s0 + b[q] >= tau, 1.0, 0.0)
        rank1 = rank1 + jnp.where(s1 < b[q], 1.0, 0.0)
    e1 = jnp.exp(s1 - b[0])
    e0n = jnp.exp(s0 - a[0]) * (1.0 / z)
    return rank1, e1, n0, e0n


def _routing_kernel(h2t_ref, wqt_ref, keys_ref, r1_ref, e1_ref, n0_ref, e0_ref,
                    q_ref, s_ref, *, tt):
    q_ref[...] = jnp.dot(wqt_ref[...], pltpu.bitcast(h2t_ref[...], BF16),
                         preferred_element_type=F32)
    n_lg = tt // LANES

    def head_body(h, carry):
        for c in range(2):
            r0 = pl.multiple_of((2 * h + c) * 128, 128)
            q_hc = q_ref[pl.ds(r0, 128), :].astype(BF16)
            s_ref[c] = jnp.dot(keys_ref[2 * h + c], q_hc, preferred_element_type=F32)
        out_rows = pl.ds(pl.multiple_of(h * 128, 128), 128)
        packed_rows = pl.ds(pl.multiple_of(h * 64, 64), 64)
        for lg in range(n_lg):
            lanes = slice(lg * LANES, (lg + 1) * LANES)
            rank1, e1, n0, e0n = _routing_tables(s_ref[0, :, lanes], s_ref[1, :, lanes])
            r1_ref[lg, packed_rows, :] = pltpu.bitcast(rank1.astype(BF16), jnp.uint32)
            e1_ref[lg, packed_rows, :] = pltpu.bitcast(e1.astype(BF16), jnp.uint32)
            n0_ref[lg, out_rows, :] = n0
            e0_ref[lg, out_rows, :] = e0n
        return carry

    lax.fori_loop(0, PEER_HEADS, head_body, 0)


def _routing(h2t, wqt, keys, *, tt):
    d, t = 2 * h2t.shape[0], h2t.shape[1]
    rows = PEER_HEADS * PEER_N_KEYS
    tab = lambda r, dt: jax.ShapeDtypeStruct((t // LANES, r, LANES), dt)
    tab_spec = lambda r: pl.BlockSpec((tt // LANES, r, LANES), lambda i: (i, 0, 0))
    return pl.pallas_call(
        functools.partial(_routing_kernel, tt=tt),
        grid=(t // tt,),
        in_specs=[
            pl.BlockSpec((d // 2, tt), lambda i: (0, i)),
            pl.BlockSpec(wqt.shape, lambda i: (0, 0)),
            pl.BlockSpec(keys.shape, lambda i: (0, 0, 0)),
        ],
        out_specs=[tab_spec(rows // 2), tab_spec(rows // 2), tab_spec(rows), tab_spec(rows)],
        out_shape=[tab(rows // 2, jnp.uint32), tab(rows // 2, jnp.uint32),
                   tab(rows, F32), tab(rows, F32)],
        scratch_shapes=[pltpu.VMEM((wqt.shape[0], tt), F32), pltpu.VMEM((2, 128, tt), F32)],
        compiler_params=pltpu.CompilerParams(
            dimension_semantics=("arbitrary",), vmem_limit_bytes=VMEM_LIMIT_BYTES),
        name="peer_routing",
    )(h2t, wqt, keys)


def _experts_kernel(xt_ref, u_ref, vt_ref, r1_ref, e1_ref, n0_ref, e0_ref, h_ref, fg_ref,
                    y_ref, acc_ref, hd_ref, at_ref, *, tt, et, n_e, n_tiles):
    g = pl.program_id(0)
    e_c = jnp.clip(g - 2, 0, n_tiles - 1) % n_e

    @pl.when(g == 0)
    def _prime():
        hd_ref[...] = jnp.zeros_like(hd_ref)
        at_ref[...] = jnp.zeros_like(at_ref)

    @pl.when(e_c == 0)
    def _new_token_tile():
        acc_ref[...] = jnp.zeros_like(acc_ref)

    stages = functools.partial(_experts_stages, xt_ref, u_ref, vt_ref, r1_ref, e1_ref, n0_ref,
                               e0_ref, acc_ref, hd_ref, at_ref, tt=tt, et=et, n_e=n_e,
                               n_tiles=n_tiles)
    pl.when(g % 2 == 0)(functools.partial(stages, slot_a=0))
    pl.when(g % 2 == 1)(functools.partial(stages, slot_a=1))

    @pl.when((g >= 2) & (e_c == n_e - 1))
    def _finish():
        y_ref[...] = _rms_norm(h_ref[...] + acc_ref[...].T, fg_ref[...])


def _experts_stages(xt_ref, u_ref, vt_ref, r1_ref, e1_ref, n0_ref, e0_ref, acc_ref, hd_ref,
                    at_ref, *, tt, et, n_e, n_tiles, slot_a):
    slot_b = 1 - slot_a
    n_ib = et // PEER_N_KEYS
    sub = 16
    e_b = jnp.clip(pl.program_id(0) - 1, 0, n_tiles - 1) % n_e

    def packed(rows):
        return slice(rows.start // 2, rows.stop // 2)

    def stage_a(rows):
        u = pltpu.bitcast(u_ref[packed(rows), :], BF16)
        xt = pltpu.bitcast(xt_ref[...], BF16)
        hd_ref[slot_a, rows, :] = jnp.dot(u, xt, preferred_element_type=F32)

    def stage_b(ii, lg):
        lanes = slice(lg * LANES, (lg + 1) * LANES)
        w = [jnp.zeros((sub, LANES), BF16) for _ in range(128 // sub)]
        for h in range(PEER_HEADS):
            grp = pl.ds(pl.multiple_of(h * 128 + e_b * n_ib + (ii // 8) * 8, 8), 8)
            n0b = jnp.broadcast_to(n0_ref[lg, grp, :][ii % 8:ii % 8 + 1, :], (sub, LANES))
            e0b = jnp.broadcast_to(e0_ref[lg, grp, :][ii % 8:ii % 8 + 1, :], (sub, LANES))
            n0b, e0b = n0b.astype(BF16), e0b.astype(BF16)
            for r in range(128 // sub):
                words = slice(h * 64 + r * 8, h * 64 + (r + 1) * 8)
                r1 = pltpu.bitcast(r1_ref[lg, words, :], BF16)
                e1 = pltpu.bitcast(e1_ref[lg, words, :], BF16)
                w[r] = w[r] + jnp.where(r1 < n0b, e1 * e0b, jnp.zeros_like(e1))
        for r in range(128 // sub):
            rows = slice(ii * 128 + r * sub, ii * 128 + (r + 1) * sub)
            x = hd_ref[slot_b, rows, lanes]
            gelu = 0.5 * x * (1.0 + lax.erf(x * math.sqrt(0.5)))
            at_ref[slot_b, rows, lanes] = gelu.astype(BF16) * w[r]

    def stage_c(rows):
        vt = pltpu.bitcast(vt_ref[packed(rows), :], BF16)
        acc_ref[rows, :] += jnp.dot(vt, at_ref[slot_a], preferred_element_type=F32)

    d = acc_ref.shape[0]
    n_lg = tt // LANES
    for k in range(n_ib):
        for lg in range(n_lg // 2):
            stage_b(k, lg)
        stage_a(slice(k * 128, (k + 1) * 128))
        for lg in range(n_lg // 2, n_lg):
            stage_b(k, lg)
        stage_c(slice(k * d // n_ib, (k + 1) * d // n_ib))


def _experts(xt, u, vt, r1, e1, n0, e0, h, fg, *, tt, et):
    d, t = 2 * xt.shape[0], xt.shape[1]
    n_exp = 2 * u.shape[0]
    rows = n0.shape[1]
    assert et % (8 * PEER_N_KEYS) == 0 and t % tt == 0 and n_exp % et == 0
    n_e = n_exp // et
    n_tiles = (t // tt) * n_e
    tile_a = lambda g: jnp.minimum(g, n_tiles - 1)
    tile_b = lambda g: jnp.clip(g - 1, 0, n_tiles - 1)
    tile_c = lambda g: jnp.clip(g - 2, 0, n_tiles - 1)
    tab_spec = lambda r: pl.BlockSpec((tt // LANES, r, LANES), lambda g: (tile_b(g) // n_e, 0, 0))
    return pl.pallas_call(
        functools.partial(_experts_kernel, tt=tt, et=et, n_e=n_e, n_tiles=n_tiles),
        grid=(n_tiles + 2,),
        in_specs=[
            pl.BlockSpec((d // 2, tt), lambda g: (0, tile_a(g) // n_e)),
            pl.BlockSpec((et // 2, d), lambda g: (tile_a(g) % n_e, 0)),
            pl.BlockSpec((d // 2, et), lambda g: (0, tile_c(g) % n_e)),
            tab_spec(rows // 2), tab_spec(rows // 2), tab_spec(rows), tab_spec(rows),
            pl.BlockSpec((tt, d), lambda g: (tile_c(g) // n_e, 0)),
            pl.BlockSpec((1, d), lambda g: (0, 0)),
        ],
        out_specs=pl.BlockSpec((tt, d), lambda g: (tile_c(g) // n_e, 0)),
        out_shape=jax.ShapeDtypeStruct((t, d), F32),
        scratch_shapes=[
            pltpu.VMEM((d, tt), F32),
            pltpu.VMEM((2, et, tt), F32),
            pltpu.VMEM((2, et, tt), BF16),
        ],
        compiler_params=pltpu.CompilerParams(
            dimension_semantics=("arbitrary",),
            vmem_limit_bytes=VMEM_LIMIT_BYTES),
        name="peer_experts",
    )(xt, u, vt, r1, e1, n0, e0, h, fg)


def _pack_row_pairs(x):
    m, n = x.shape
    return lax.bitcast_convert_type(jnp.swapaxes(x.reshape(m // 2, 2, n), -1, -2), jnp.uint32)


def _peer_and_final(h, h2t, wqt, keys, u, vt, fg, *, tt_route, tt, et):
    r1, e1, n0, e0 = _routing(h2t, wqt, keys, tt=tt_route)
    return _experts(h2t, u, vt, r1, e1, n0, e0, h, fg, tt=tt, et=et)


def kernel(x_prompt, x_sample, state_ret, cache_swa_k, cache_swa_v, norm1_g, w_in, swa_sinks, w_out, norm2_g, peer_w_q, peer_sub_keys, peer_u, peer_v, final_g):
    batch, seq, d = x_prompt.shape
    n_s = x_sample.shape[0]
    g1 = norm1_g.reshape(1, d)
    g2 = norm2_g.reshape(1, d)
    fg = final_g.reshape(1, d)
    win = w_in.astype(BF16)
    wout = w_out.astype(BF16)
    sinks = swa_sinks.astype(F32).reshape(SWA_Q_HEADS, 1)
    wqt = peer_w_q.T.astype(BF16)
    keys = peer_sub_keys.reshape(PEER_HEADS * 2, PEER_N_KEYS, -1).astype(BF16)
    u = _pack_row_pairs(peer_u.astype(BF16))
    vt = _pack_row_pairs(peer_v.T.astype(BF16))
    half = RET_DK // 2
    inv = 1.0 / (ROPE_BASE ** jnp.linspace(0.0, 1.0, half, dtype=F32))
    inv = jnp.repeat(inv, 2).reshape(1, RET_DK)

    h_p, h2t_p, st_p, k_p, v_p = _prompt_mixer(x_prompt, g1, win, sinks, wout, g2, inv, tq=512)
    y_p = _peer_and_final(h_p, h2t_p, wqt, keys, u, vt, fg, tt_route=256, tt=512, et=1024)

    xs = x_sample.reshape(n_s, d)
    proj = _sample_proj(xs, g1, win, inv)
    st_s, k_s, v_s, o2, sres = _sample_state(
        proj, state_ret, cache_swa_k.reshape(n_s, WINDOW, 128),
        cache_swa_v.reshape(n_s, WINDOW, 128), sinks, group=8)
    h_s, h2t_s = _sample_finish(xs, proj, o2, sres.reshape(n_s, SWA_Q_HEADS * 128), wout, g2)
    y_s = _peer_and_final(h_s, h2t_s, wqt, keys, u, vt, fg, tt_route=128, tt=128, et=1024)

    kv_shape = (WINDOW, 2, SWA_HD)
    return (y_p.reshape(batch, seq, d), y_s.reshape(n_s, 1, d), st_p,
            k_p.reshape(batch, *kv_shape), v_p.reshape(batch, *kv_shape),
            st_s, k_s.reshape(n_s, *kv_shape), v_s.reshape(n_s, *kv_shape))
```

```python
import functools
import math

import jax
import jax.numpy as jnp
import numpy as np
from jax import lax
from jax.experimental import pallas as pl
from jax.experimental.pallas import tpu as pltpu

F32 = jnp.float32
BF16 = jnp.bfloat16

D_MODEL = 1024
SEQ = 8192
PAST_LEN = 8192
RET_HEADS = 4
RET_DK = 128
CHUNK = 128
ROPE_BASE = 10000.0
SWA_Q_HEADS = 8
SWA_HD = 64
WINDOW = 128
D_IN = 2816
OFF_RQ, OFF_RK, OFF_RV, OFF_RG, OFF_SQ, OFF_SK, OFF_SV = 0, 512, 1024, 1536, 2048, 2560, 2688
PEER_HEADS = 8
PEER_N_KEYS = 128
PEER_N_EXPERTS = PEER_N_KEYS * PEER_N_KEYS
PEER_TOPK = 16
RMS_EPS = 1e-6
LANES = 128

LOG_DECAY = tuple(math.log(1.0 - 2.0 ** (-5.0 - h)) for h in range(RET_HEADS))

VMEM_LIMIT_BYTES = 56 * 1024 * 1024

NEG_INF = float("-inf")
NT_DIMS = (((1,), (1,)), ((), ()))
TN_DIMS = (((0,), (0,)), ((), ()))


def _rms_norm(x, g):
    return x * lax.rsqrt(jnp.mean(x * x, axis=-1, keepdims=True) + RMS_EPS) * g


def _rotation_tables(pos, inv):
    ang = pos * inv
    cos = jnp.cos(ang)
    sin = jnp.sin(ang)
    lane = lax.broadcasted_iota(jnp.int32, ang.shape, 1)
    even = (lane & 1) == 0
    return cos, jnp.where(even, -sin, sin), even


def _rotate_pairs(x, cos, sin_signed, even):
    partner = jnp.where(even, pltpu.roll(x, LANES - 1, 1), pltpu.roll(x, 1, 1))
    return x * cos + partner * sin_signed


def _silu(g):
    return g * jax.nn.sigmoid(g)


def _head_half_variants(a, lo):
    rolled = pltpu.roll(a, 64, 1)
    zero = jnp.zeros_like(a)
    return (jnp.where(lo, a, zero), jnp.where(lo, zero, rolled),
            jnp.where(lo, rolled, zero), jnp.where(lo, zero, a))


def _prompt_mixer_kernel(x_ref, g1_ref, win_ref, sinks_ref, wout_ref, g2_ref, inv_ref,
                         h_ref, h2t_ref, st_ref, kp_ref, vp_ref,
                         proj_ref, state_ref, kcat_ref, vcat_ref, mix_ref,
                         dmask_ref, qdec_ref, kdec_ref, *, tq):
    s = pl.program_id(1)
    n_chunks = tq // CHUNK

    @pl.when(s == 0)
    def _init():
        state_ref[...] = jnp.zeros_like(state_ref)
        kcat_ref[...] = jnp.zeros_like(kcat_ref)
        vcat_ref[...] = jnp.zeros_like(vcat_ref)
        t_row = lax.broadcasted_iota(jnp.int32, (CHUNK, CHUNK), 0).astype(F32)
        t_col = lax.broadcasted_iota(jnp.int32, (CHUNK, CHUNK), 1).astype(F32)
        diff = t_row - t_col
        causal = diff >= 0
        for hh in range(RET_HEADS):
            lg = LOG_DECAY[hh]
            dmask_ref[hh] = jnp.where(causal, jnp.exp(jnp.where(causal, diff, 0.0) * lg), 0.0)
            qdec_ref[hh] = jnp.exp((t_row + 1.0) * lg)
            kdec_ref[hh] = jnp.exp((CHUNK - 1.0 - t_row) * lg)

    x = x_ref[...]
    xn = _rms_norm(x, g1_ref[...]).astype(BF16)
    proj_ref[...] = jnp.dot(xn, pltpu.bitcast(win_ref[...], BF16), preferred_element_type=F32)

    def chunk_body(c, carry):
        r0 = pl.multiple_of(c * CHUNK, CHUNK)
        rows = pl.ds(r0, CHUNK)
        chunk_idx = s * n_chunks + c
        t_local = lax.broadcasted_iota(jnp.int32, (CHUNK, 1), 0)
        pos = (chunk_idx * CHUNK + t_local).astype(F32)
        cos, sin_signed, even = _rotation_tables(pos, inv_ref[...])

        ret_heads = range(RET_HEADS)
        swa_heads = range(SWA_Q_HEADS)

        col = lambda off, hh: slice(off + hh * 128, off + (hh + 1) * 128)
        k_ret = [_rotate_pairs(proj_ref[rows, col(OFF_RK, hh)], cos, sin_signed, even)
                 * (RET_DK ** -0.5) for hh in ret_heads]
        qb = [_rotate_pairs(proj_ref[rows, col(OFF_RQ, hh)], cos, sin_signed, even).astype(BF16)
              for hh in ret_heads]
        kb = [k.astype(BF16) for k in k_ret]
        k_dec = [(k_ret[hh] * kdec_ref[hh]).astype(BF16) for hh in ret_heads]
        vb = [proj_ref[rows, col(OFF_RV, hh)].astype(BF16) for hh in ret_heads]
        kc = proj_ref[rows, OFF_SK:OFF_SK + 128]
        vc = proj_ref[rows, OFF_SV:OFF_SV + 128]
        kp_ref[...] = kc
        vp_ref[...] = vc
        lane = lax.broadcasted_iota(jnp.int32, (CHUNK, LANES), 1)
        lo = lane < 64
        for idx, (kvar, vvar) in enumerate(zip(_head_half_variants(kc, lo),
                                                _head_half_variants(vc, lo))):
            kcat_ref[idx, CHUNK:2 * CHUNK, :] = kvar.astype(BF16)
            vcat_ref[idx, CHUNK:2 * CHUNK, :] = vvar.astype(BF16)
        q_swa = [proj_ref[rows, col(OFF_SQ, i)].astype(BF16) for i in range(SWA_Q_HEADS // 2)]
        kv_idx = lambda head: (head // 4) * 2 + head % 2

        ret_scores = [lax.dot_general(qb[hh], kb[hh], NT_DIMS, preferred_element_type=F32)
                      for hh in ret_heads]
        ret_cross = [jnp.dot(qb[hh], state_ref[hh].astype(BF16), preferred_element_type=F32)
                     for hh in ret_heads]
        ret_kv = [lax.dot_general(k_dec[hh], vb[hh], TN_DIMS, preferred_element_type=F32)
                  for hh in ret_heads]
        swa_scores = [lax.dot_general(q_swa[head // 2], kcat_ref[kv_idx(head)], NT_DIMS,
                                      preferred_element_type=F32) for head in swa_heads]

        for hh in ret_heads:
            state_ref[hh] = state_ref[hh] * math.exp(CHUNK * LOG_DECAY[hh]) + ret_kv[hh]
        ret_p = [(ret_scores[hh] * dmask_ref[hh]).astype(BF16) for hh in ret_heads]
        q_row = lax.broadcasted_iota(jnp.int32, (CHUNK, 2 * CHUNK), 0)
        k_col = lax.broadcasted_iota(jnp.int32, (CHUNK, 2 * CHUNK), 1)
        valid = (k_col >= q_row + 1) & (k_col <= q_row + WINDOW)
        valid = valid & ((k_col >= CHUNK) | (chunk_idx > 0))
        swa_p, swa_den = [], []
        for head in swa_heads:
            sink = sinks_ref[head:head + 1, :]
            sc = jnp.where(valid, swa_scores[head] * (SWA_HD ** -0.5), NEG_INF)
            m = jnp.maximum(jnp.max(sc, axis=-1, keepdims=True), sink)
            p = jnp.exp(sc - m)
            swa_den.append(jnp.sum(p, axis=-1, keepdims=True) + jnp.exp(sink - m))
            swa_p.append(p.astype(BF16))

        ret_o = [jnp.dot(ret_p[hh], vb[hh], preferred_element_type=F32) for hh in ret_heads]
        swa_o = [jnp.dot(swa_p[head], vcat_ref[kv_idx(head)], preferred_element_type=F32)
                 for head in swa_heads]

        for hh in ret_heads:
            o = ret_o[hh] + ret_cross[hh] * qdec_ref[hh]
            o = o * lax.rsqrt(jnp.mean(o * o, axis=-1, keepdims=True) + RMS_EPS)
            gate = proj_ref[rows, col(OFF_RG, hh)]
            mix_ref[rows, col(0, hh)] = (o * _silu(gate)).astype(BF16)
        for i in range(SWA_Q_HEADS // 2):
            pair = swa_o[2 * i] / swa_den[2 * i] + swa_o[2 * i + 1] / swa_den[2 * i + 1]
            mix_ref[rows, col(512, i)] = pair.astype(BF16)
        for idx in range(4):
            kcat_ref[idx, 0:CHUNK, :] = kcat_ref[idx, CHUNK:2 * CHUNK, :]
            vcat_ref[idx, 0:CHUNK, :] = vcat_ref[idx, CHUNK:2 * CHUNK, :]
        return carry

    lax.fori_loop(0, n_chunks, chunk_body, 0)

    h = x + jnp.dot(mix_ref[...], pltpu.bitcast(wout_ref[...], BF16),
                             preferred_element_type=F32)
    h_ref[...] = h
    h2t_ref[...] = pltpu.bitcast(_rms_norm(h, g2_ref[...]).T.astype(BF16), jnp.uint32)
    st_ref[...] = state_ref[...]


def _prompt_mixer(x, g1, win, sinks, wout, g2, inv, *, tq):
    batch, seq, d = x.shape
    ns = seq // tq
    full = lambda shape: pl.BlockSpec(shape, lambda b, s: (0,) * len(shape))
    return pl.pallas_call(
        functools.partial(_prompt_mixer_kernel, tq=tq),
        grid=(batch, ns),
        in_specs=[
            pl.BlockSpec((None, tq, d), lambda b, s: (b, s, 0)),
            full((1, d)), full((d // 2, D_IN)), full((SWA_Q_HEADS, 1)), full((d // 2, d)),
            full((1, d)),
            full((1, LANES)),
        ],
        out_specs=[
            pl.BlockSpec((tq, d), lambda b, s: (b * ns + s, 0)),
            pl.BlockSpec((d // 2, tq), lambda b, s: (0, b * ns + s)),
            pl.BlockSpec((None, RET_HEADS, 128, 128), lambda b, s: (b, 0, 0, 0)),
            pl.BlockSpec((None, WINDOW, 128), lambda b, s: (b, 0, 0)),
            pl.BlockSpec((None, WINDOW, 128), lambda b, s: (b, 0, 0)),
        ],
        out_shape=[
            jax.ShapeDtypeStruct((batch * seq, d), F32),
            jax.ShapeDtypeStruct((d // 2, batch * seq), jnp.uint32),
            jax.ShapeDtypeStruct((batch, RET_HEADS, 128, 128), F32),
            jax.ShapeDtypeStruct((batch, WINDOW, 128), F32),
            jax.ShapeDtypeStruct((batch, WINDOW, 128), F32),
        ],
        scratch_shapes=[
            pltpu.VMEM((tq, D_IN), F32),
            pltpu.VMEM((RET_HEADS, 128, 128), F32),
            pltpu.VMEM((4, 2 * CHUNK, LANES), BF16),
            pltpu.VMEM((4, 2 * CHUNK, LANES), BF16),
            pltpu.VMEM((tq, d), BF16),
            pltpu.VMEM((RET_HEADS, CHUNK, CHUNK), F32),
            pltpu.VMEM((RET_HEADS, CHUNK, CHUNK), F32),
            pltpu.VMEM((RET_HEADS, CHUNK, CHUNK), F32),
        ],
        compiler_params=pltpu.CompilerParams(
            dimension_semantics=("arbitrary", "arbitrary"),
            vmem_limit_bytes=VMEM_LIMIT_BYTES),
        name="prompt_mixer",
    )(x, g1, win, sinks, wout, g2, inv)


def _sample_proj_kernel(x_ref, g1_ref, win_ref, inv_ref, proj_ref):
    xn = _rms_norm(x_ref[...], g1_ref[...]).astype(BF16)
    proj = jnp.dot(xn, pltpu.bitcast(win_ref[...], BF16), preferred_element_type=F32)
    proj_ref[...] = proj
    pos = jnp.full((1, 1), float(PAST_LEN), F32)
    cos, sin_signed, even = _rotation_tables(pos * jnp.ones((1, LANES), F32), inv_ref[...])
    even_full = jnp.broadcast_to(even, (x_ref.shape[0], LANES))
    for hh in range(RET_HEADS):
        ql = slice(OFF_RQ + hh * 128, OFF_RQ + (hh + 1) * 128)
        kl = slice(OFF_RK + hh * 128, OFF_RK + (hh + 1) * 128)
        proj_ref[:, ql] = _rotate_pairs(proj[:, ql], cos, sin_signed, even_full)
        proj_ref[:, kl] = _rotate_pairs(proj[:, kl], cos, sin_signed, even_full) * (RET_DK ** -0.5)


def _sample_proj(x, g1, win, inv):
    n = x.shape[0]
    return pl.pallas_call(
        _sample_proj_kernel,
        out_shape=jax.ShapeDtypeStruct((n, D_IN), F32),
        compiler_params=pltpu.CompilerParams(vmem_limit_bytes=VMEM_LIMIT_BYTES),
        name="sample_proj",
    )(x, g1, win, inv)


def _sample_state_kernel(proj_ref, st_ref, ck_ref, cv_ref, sinks_ref,
                         stn_ref, kn_ref, vn_ref, o2_ref, sres_ref, *, group):
    row8 = lax.broadcasted_iota(jnp.int32, (8, LANES), 0)
    lane8 = lax.broadcasted_iota(jnp.int32, (8, LANES), 1)
    lo1 = lax.broadcasted_iota(jnp.int32, (1, LANES), 1) < 64
    sink_col = sinks_ref[...]

    def sample_body(b, carry):
        row = pl.ds(b, 1)
        for hh in range(RET_HEADS):
            gamma = math.exp(LOG_DECAY[hh])
            q = proj_ref[row, OFF_RQ + hh * 128:OFF_RQ + (hh + 1) * 128]
            k = proj_ref[row, OFF_RK + hh * 128:OFF_RK + (hh + 1) * 128]
            v = proj_ref[row, OFF_RV + hh * 128:OFF_RV + (hh + 1) * 128]
            state = st_ref[b, hh]
            q8 = jnp.broadcast_to(q, (8, LANES)).astype(BF16)
            o2 = jnp.dot(q8, state.astype(BF16), preferred_element_type=F32)
            o2_ref[row, hh * 128:(hh + 1) * 128] = o2[0:1, :]
            k8 = jnp.where(row8 == 0, jnp.broadcast_to(k, (8, LANES)), 0.0).astype(BF16)
            v8 = jnp.broadcast_to(v, (8, LANES)).astype(BF16)
            outer = lax.dot_general(k8, v8, TN_DIMS, preferred_element_type=F32)
            stn_ref[b, hh] = state * gamma + outer
        kn_ref[b, 0:WINDOW - 1, :] = ck_ref[b, 1:WINDOW, :]
        vn_ref[b, 0:WINDOW - 1, :] = cv_ref[b, 1:WINDOW, :]
        kn_ref[b, WINDOW - 1:WINDOW, :] = proj_ref[row, OFF_SK:OFF_SK + 128]
        vn_ref[b, WINDOW - 1:WINDOW, :] = proj_ref[row, OFF_SV:OFF_SV + 128]
        kwin = kn_ref[b].astype(BF16)
        vwin = vn_ref[b].astype(BF16)
        q_pad = jnp.zeros((8, LANES), F32)
        for qb_idx in range(SWA_Q_HEADS // 2):
            g = qb_idx // 2
            blk = proj_ref[row, OFF_SQ + qb_idx * 128:OFF_SQ + (qb_idx + 1) * 128]
            rolled = pltpu.roll(blk, 64, 1)
            zero = jnp.zeros_like(blk)
            if g == 0:
                first, second = jnp.where(lo1, blk, zero), jnp.where(lo1, rolled, zero)
            else:
                first, second = jnp.where(lo1, zero, rolled), jnp.where(lo1, zero, blk)
            q_pad = jnp.where(row8 == 2 * qb_idx, jnp.broadcast_to(first, (8, LANES)), q_pad)
            q_pad = jnp.where(row8 == 2 * qb_idx + 1, jnp.broadcast_to(second, (8, LANES)), q_pad)
        sc = lax.dot_general(q_pad.astype(BF16), kwin, NT_DIMS,
                             preferred_element_type=F32) * (SWA_HD ** -0.5)
        m = jnp.maximum(jnp.max(sc, axis=-1, keepdims=True), sink_col)
        p = jnp.exp(sc - m)
        den = jnp.sum(p, axis=-1, keepdims=True) + jnp.exp(sink_col - m)
        sres_ref[b] = jnp.dot(p.astype(BF16), vwin, preferred_element_type=F32) / den
        return carry

    del lane8
    for b in range(group):
        sample_body(b, 0)


def _sample_state(proj, state, ck, cv, sinks, *, group):
    n = proj.shape[0]
    return pl.pallas_call(
        functools.partial(_sample_state_kernel, group=group),
        grid=(n // group,),
        in_specs=[
            pl.BlockSpec((group, D_IN), lambda i: (i, 0)),
            pl.BlockSpec((group, RET_HEADS, 128, 128), lambda i: (i, 0, 0, 0)),
            pl.BlockSpec((group, WINDOW, 128), lambda i: (i, 0, 0)),
            pl.BlockSpec((group, WINDOW, 128), lambda i: (i, 0, 0)),
            pl.BlockSpec((SWA_Q_HEADS, 1), lambda i: (0, 0)),
        ],
        out_specs=[
            pl.BlockSpec((group, RET_HEADS, 128, 128), lambda i: (i, 0, 0, 0)),
            pl.BlockSpec((group, WINDOW, 128), lambda i: (i, 0, 0)),
            pl.BlockSpec((group, WINDOW, 128), lambda i: (i, 0, 0)),
            pl.BlockSpec((group, 512), lambda i: (i, 0)),
            pl.BlockSpec((group, SWA_Q_HEADS, 128), lambda i: (i, 0, 0)),
        ],
        out_shape=[
            jax.ShapeDtypeStruct((n, RET_HEADS, 128, 128), F32),
            jax.ShapeDtypeStruct((n, WINDOW, 128), F32),
            jax.ShapeDtypeStruct((n, WINDOW, 128), F32),
            jax.ShapeDtypeStruct((n, 512), F32),
            jax.ShapeDtypeStruct((n, SWA_Q_HEADS, 128), F32),
        ],
        compiler_params=pltpu.CompilerParams(
            dimension_semantics=("arbitrary",), vmem_limit_bytes=VMEM_LIMIT_BYTES),
        name="sample_state",
    )(proj, state, ck, cv, sinks)


def _sample_finish_kernel(x_ref, proj_ref, o2_ref, sres_ref, wout_ref, g2_ref,
                          h_ref, h2t_ref, mix_ref):
    n = x_ref.shape[0]
    for hh in range(RET_HEADS):
        gamma = math.exp(LOG_DECAY[hh])
        lanes = slice(hh * 128, (hh + 1) * 128)
        q = proj_ref[:, OFF_RQ + hh * 128:OFF_RQ + (hh + 1) * 128]
        k = proj_ref[:, OFF_RK + hh * 128:OFF_RK + (hh + 1) * 128]
        v = proj_ref[:, OFF_RV + hh * 128:OFF_RV + (hh + 1) * 128]
        gate = proj_ref[:, OFF_RG + hh * 128:OFF_RG + (hh + 1) * 128]
        qk = jnp.sum(q * k, axis=-1, keepdims=True)
        o = qk * v + o2_ref[:, lanes] * gamma
        o = o * lax.rsqrt(jnp.mean(o * o, axis=-1, keepdims=True) + RMS_EPS)
        mix_ref[:, lanes] = (o * _silu(gate)).astype(BF16)
    lo = lax.broadcasted_iota(jnp.int32, (n, LANES), 1) < 64
    for qb_idx in range(SWA_Q_HEADS // 2):
        r0 = sres_ref[:, (2 * qb_idx) * 128:(2 * qb_idx + 1) * 128]
        r1 = sres_ref[:, (2 * qb_idx + 1) * 128:(2 * qb_idx + 2) * 128]
        if qb_idx // 2 == 0:
            blk = jnp.where(lo, r0, pltpu.roll(r1, 64, 1))
        else:
            blk = jnp.where(lo, pltpu.roll(r0, 64, 1), r1)
        mix_ref[:, 512 + qb_idx * 128:512 + (qb_idx + 1) * 128] = blk.astype(BF16)
    h = x_ref[...] + jnp.dot(mix_ref[...], pltpu.bitcast(wout_ref[...], BF16),
                             preferred_element_type=F32)
    h_ref[...] = h
    h2t_ref[...] = pltpu.bitcast(_rms_norm(h, g2_ref[...]).T.astype(BF16), jnp.uint32)


def _sample_finish(x, proj, o2, sres, wout, g2):
    n, d = x.shape
    return pl.pallas_call(
        _sample_finish_kernel,
        out_shape=[jax.ShapeDtypeStruct((n, d), F32),
                   jax.ShapeDtypeStruct((d // 2, n), jnp.uint32)],
        scratch_shapes=[pltpu.VMEM((n, d), BF16)],
        compiler_params=pltpu.CompilerParams(vmem_limit_bytes=VMEM_LIMIT_BYTES),
        name="sample_finish",
    )(x, proj, o2, sres, wout, g2)


def _top16_desc(s):
    work = s
    rows = []
    for p in range(PEER_TOPK):
        m = jnp.max(work, axis=0, keepdims=True)
        rows.append(m)
        if p + 1 < PEER_TOPK:
            work = jnp.where(work == m, NEG_INF, work)
    return rows


def _stack8(rows, row8):
    out = jnp.zeros(row8.shape, F32)
    for r, v in enumerate(rows):
        out = jnp.where(row8 == r, jnp.broadcast_to(v, row8.shape), out)
    return out


def _routing_tables(s0, s1):
    a = _top16_desc(s0)
    b = _top16_desc(s1)
    row8 = lax.broadcasted_iota(jnp.int32, (8, s0.shape[1]), 0)
    a_lo, a_hi = _stack8(a[:8], row8), _stack8(a[8:], row8)
    b_lo, b_hi = _stack8(b[:8], row8), _stack8(b[8:], row8)
    neg = jnp.full(row8.shape, NEG_INF, F32)
    cands = [
        a[0] + b_lo, a[0] + b_hi, a[1] + b_lo,
        jnp.where(row8 < 5, a[2] + b_lo, neg),
        jnp.where(row8 < 4, a[3] + b_lo, neg),
        b[0] + a_hi,
        jnp.where(row8 >= 4, b[0] + a_lo, neg),
        jnp.where(row8 >= 4, b[1] + a_lo, neg),
        jnp.where(row8 == 4, b[2] + a_lo, neg),
    ]
    work = cands
    tau = None
    for it in range(PEER_TOPK):
        m = functools.reduce(jnp.maximum, work)
        tau = jnp.max(m, axis=0, keepdims=True)
        if it + 1 < PEER_TOPK:
            work = [jnp.where(w == tau, NEG_INF, w) for w in work]
    top = a[0] + b[0]
    z = functools.reduce(
        lambda x, y: x + y,
        [jnp.sum(jnp.where(c >= tau, jnp.exp(c - top), 0.0), axis=0, keepdims=True) for c in cands])
    n0 = jnp.zeros(s0.shape, F32)
    rank1 = jnp.zeros(s1.shape, F32)
    for q in range(PEER_TOPK):
        n0 = n0 + jnp.where(s0 + b[q] >= tau, 1.0, 0.0)
        rank1 = rank1 + jnp.where(s1 < b[q], 1.0, 0.0)
    e1 = jnp.exp(s1 - b[0])
    e0n = jnp.exp(s0 - a[0]) * (1.0 / z)
    return rank1, e1, n0, e0n


def _routing_kernel(h2t_ref, wqt_ref, keys_ref, r1_ref, e1_ref, n0_ref, e0_ref,
                    q_ref, s_ref, *, tt):
    q_ref[...] = jnp.dot(pltpu.bitcast(wqt_ref[...], BF16), pltpu.bitcast(h2t_ref[...], BF16),
                         preferred_element_type=F32)
    n_lg = tt // LANES

    def head_body(h, carry):
        for c in range(2):
            r0 = pl.multiple_of((2 * h + c) * 128, 128)
            q_hc = q_ref[pl.ds(r0, 128), :].astype(BF16)
            key_words = keys_ref[pl.ds(pl.multiple_of((2 * h + c) * 64, 64), 64), :]
            s_ref[c] = jnp.dot(pltpu.bitcast(key_words, BF16), q_hc, preferred_element_type=F32)
        out_rows = pl.ds(pl.multiple_of(h * 128, 128), 128)
        packed_rows = pl.ds(pl.multiple_of(h * 64, 64), 64)
        for lg in range(n_lg):
            lanes = slice(lg * LANES, (lg + 1) * LANES)
            rank1, e1, n0, e0n = _routing_tables(s_ref[0, :, lanes], s_ref[1, :, lanes])
            r1_ref[lg, packed_rows, :] = pltpu.bitcast(rank1.astype(BF16), jnp.uint32)
            e1_ref[lg, packed_rows, :] = pltpu.bitcast(e1.astype(BF16), jnp.uint32)
            n0_ref[lg, out_rows, :] = n0
            e0_ref[lg, out_rows, :] = e0n
        return carry

    lax.fori_loop(0, PEER_HEADS, head_body, 0)


def _routing(h2t, wqt, keys, *, tt):
    d, t = 2 * h2t.shape[0], h2t.shape[1]
    rows = PEER_HEADS * PEER_N_KEYS
    tab = lambda r, dt: jax.ShapeDtypeStruct((t // LANES, r, LANES), dt)
    tab_spec = lambda r: pl.BlockSpec((tt // LANES, r, LANES), lambda i: (i, 0, 0))
    return pl.pallas_call(
        functools.partial(_routing_kernel, tt=tt),
        grid=(t // tt,),
        in_specs=[
            pl.BlockSpec((d // 2, tt), lambda i: (0, i)),
            pl.BlockSpec(wqt.shape, lambda i: (0, 0)),
            pl.BlockSpec(keys.shape, lambda i: (0, 0)),
        ],
        out_specs=[tab_spec(rows // 2), tab_spec(rows // 2), tab_spec(rows), tab_spec(rows)],
        out_shape=[tab(rows // 2, jnp.uint32), tab(rows // 2, jnp.uint32),
                   tab(rows, F32), tab(rows, F32)],
        scratch_shapes=[pltpu.VMEM((2 * wqt.shape[0], tt), F32), pltpu.VMEM((2, 128, tt), F32)],
        compiler_params=pltpu.CompilerParams(
            dimension_semantics=("arbitrary",), vmem_limit_bytes=VMEM_LIMIT_BYTES),
        name="peer_routing",
    )(h2t, wqt, keys)


def _experts_kernel(xt_ref, u_ref, vt_ref, r1_ref, e1_ref, n0_ref, e0_ref, h_ref, fg_ref,
                    y_ref, acc_ref, hd_ref, at_ref, *, tt, et, n_e, n_tiles):
    g = pl.program_id(0)
    e_c = jnp.clip(g - 2, 0, n_tiles - 1) % n_e

    @pl.when(g == 0)
    def _prime():
        hd_ref[...] = jnp.zeros_like(hd_ref)
        at_ref[...] = jnp.zeros_like(at_ref)

    @pl.when(e_c == 0)
    def _new_token_tile():
        acc_ref[...] = jnp.zeros_like(acc_ref)

    stages = functools.partial(_experts_stages, xt_ref, u_ref, vt_ref, r1_ref, e1_ref, n0_ref,
                               e0_ref, acc_ref, hd_ref, at_ref, tt=tt, et=et, n_e=n_e,
                               n_tiles=n_tiles)
    pl.when(g % 2 == 0)(functools.partial(stages, slot_a=0))
    pl.when(g % 2 == 1)(functools.partial(stages, slot_a=1))

    @pl.when((g >= 2) & (e_c == n_e - 1))
    def _finish():
        y_ref[...] = _rms_norm(h_ref[...] + acc_ref[...].T, fg_ref[...])


def _experts_stages(xt_ref, u_ref, vt_ref, r1_ref, e1_ref, n0_ref, e0_ref, acc_ref, hd_ref,
                    at_ref, *, tt, et, n_e, n_tiles, slot_a):
    slot_b = 1 - slot_a
    n_ib = et // PEER_N_KEYS
    sub = 16
    e_b = jnp.clip(pl.program_id(0) - 1, 0, n_tiles - 1) % n_e

    def packed(rows):
        return slice(rows.start // 2, rows.stop // 2)

    def stage_a(rows):
        u = pltpu.bitcast(u_ref[packed(rows), :], BF16)
        xt = pltpu.bitcast(xt_ref[...], BF16)
        hd_ref[slot_a, rows, :] = jnp.dot(u, xt, preferred_element_type=F32)

    def stage_b(ii, lg):
        lanes = slice(lg * LANES, (lg + 1) * LANES)
        w = [jnp.zeros((sub, LANES), BF16) for _ in range(128 // sub)]
        for h in range(PEER_HEADS):
            grp = pl.ds(pl.multiple_of(h * 128 + e_b * n_ib + (ii // 8) * 8, 8), 8)
            n0b = jnp.broadcast_to(n0_ref[lg, grp, :][ii % 8:ii % 8 + 1, :], (sub, LANES))
            e0b = jnp.broadcast_to(e0_ref[lg, grp, :][ii % 8:ii % 8 + 1, :], (sub, LANES))
            n0b, e0b = n0b.astype(BF16), e0b.astype(BF16)
            for r in range(128 // sub):
                words = slice(h * 64 + r * 8, h * 64 + (r + 1) * 8)
                r1 = pltpu.bitcast(r1_ref[lg, words, :], BF16)
                e1 = pltpu.bitcast(e1_ref[lg, words, :], BF16)
                w[r] = w[r] + jnp.where(r1 < n0b, e1 * e0b, jnp.zeros_like(e1))
        for r in range(128 // sub):
            rows = slice(ii * 128 + r * sub, ii * 128 + (r + 1) * sub)
            x = hd_ref[slot_b, rows, lanes]
            gelu = 0.5 * x * (1.0 + lax.erf(x * math.sqrt(0.5)))
            at_ref[slot_b, rows, lanes] = gelu.astype(BF16) * w[r]

    def stage_c(rows):
        vt = pltpu.bitcast(vt_ref[packed(rows), :], BF16)
        acc_ref[rows, :] += jnp.dot(vt, at_ref[slot_a], preferred_element_type=F32)

    d = acc_ref.shape[0]
    n_lg = tt // LANES
    for k in range(n_ib):
        for lg in range(n_lg // 2):
            stage_b(k, lg)
        stage_a(slice(k * 128, (k + 1) * 128))
        for lg in range(n_lg // 2, n_lg):
            stage_b(k, lg)
        stage_c(slice(k * d // n_ib, (k + 1) * d // n_ib))


def _experts(xt, u, vt, r1, e1, n0, e0, h, fg, *, tt, et):
    d, t = 2 * xt.shape[0], xt.shape[1]
    n_exp = 2 * u.shape[0]
    rows = n0.shape[1]
    assert et % (8 * PEER_N_KEYS) == 0 and t % tt == 0 and n_exp % et == 0
    n_e = n_exp // et
    n_tiles = (t // tt) * n_e
    tile_a = lambda g: jnp.minimum(g, n_tiles - 1)
    tile_b = lambda g: jnp.clip(g - 1, 0, n_tiles - 1)
    tile_c = lambda g: jnp.clip(g - 2, 0, n_tiles - 1)
    tab_spec = lambda r: pl.BlockSpec((tt // LANES, r, LANES), lambda g: (tile_b(g) // n_e, 0, 0))
    return pl.pallas_call(
        functools.partial(_experts_kernel, tt=tt, et=et, n_e=n_e, n_tiles=n_tiles),
        grid=(n_tiles + 2,),
        in_specs=[
            pl.BlockSpec((d // 2, tt), lambda g: (0, tile_a(g) // n_e)),
            pl.BlockSpec((et // 2, d), lambda g: (tile_a(g) % n_e, 0)),
            pl.BlockSpec((d // 2, et), lambda g: (0, tile_c(g) % n_e)),
            tab_spec(rows // 2), tab_spec(rows // 2), tab_spec(rows), tab_spec(rows),
            pl.BlockSpec((tt, d), lambda g: (tile_c(g) // n_e, 0)),
            pl.BlockSpec((1, d), lambda g: (0, 0)),
        ],
        out_specs=pl.BlockSpec((tt, d), lambda g: (tile_c(g) // n_e, 0)),
        out_shape=jax.ShapeDtypeStruct((t, d), F32),
        scratch_shapes=[
            pltpu.VMEM((d, tt), F32),
            pltpu.VMEM((2, et, tt), F32),
            pltpu.VMEM((2, et, tt), BF16),
        ],
        compiler_params=pltpu.CompilerParams(
            dimension_semantics=("arbitrary",),
            vmem_limit_bytes=VMEM_LIMIT_BYTES),
        name="peer_experts",
    )(xt, u, vt, r1, e1, n0, e0, h, fg)


def _pack_kernel(x_ref, o_ref, *, transpose):
    x = x_ref[...]
    if transpose:
        x = x.T
    o_ref[...] = pltpu.bitcast(x.astype(BF16), jnp.uint32)


def _pack_bf16(x, *, transpose=False, block_rows=1024):
    m, n = x.shape
    bm = min(block_rows, m)
    assert m % bm == 0
    if transpose:
        out_shape, out_spec = (n // 2, m), pl.BlockSpec((n // 2, bm), lambda i: (0, i))
    else:
        out_shape, out_spec = (m // 2, n), pl.BlockSpec((bm // 2, n), lambda i: (i, 0))
    return pl.pallas_call(
        functools.partial(_pack_kernel, transpose=transpose),
        grid=(m // bm,),
        in_specs=[pl.BlockSpec((bm, n), lambda i: (i, 0))],
        out_specs=out_spec,
        out_shape=jax.ShapeDtypeStruct(out_shape, jnp.uint32),
        compiler_params=pltpu.CompilerParams(
            dimension_semantics=("arbitrary",), vmem_limit_bytes=VMEM_LIMIT_BYTES),
        name="pack_bf16",
    )(x)


def _peer_and_final(h, h2t, wqt, keys, u, vt, fg, *, tt_route, tt, et):
    r1, e1, n0, e0 = _routing(h2t, wqt, keys, tt=tt_route)
    return _experts(h2t, u, vt, r1, e1, n0, e0, h, fg, tt=tt, et=et)


def kernel(x_prompt, x_sample, state_ret, cache_swa_k, cache_swa_v, norm1_g, w_in, swa_sinks, w_out, norm2_g, peer_w_q, peer_sub_keys, peer_u, peer_v, final_g):
    batch, seq, d = x_prompt.shape
    n_s = x_sample.shape[0]
    g1 = norm1_g.reshape(1, d)
    g2 = norm2_g.reshape(1, d)
    fg = final_g.reshape(1, d)
    sinks = swa_sinks.astype(F32).reshape(SWA_Q_HEADS, 1)
    win = _pack_bf16(w_in)
    wout = _pack_bf16(w_out)
    wqt = _pack_bf16(peer_w_q, transpose=True)
    keys = _pack_bf16(peer_sub_keys.reshape(PEER_HEADS * 2 * PEER_N_KEYS, -1))
    u = _pack_bf16(peer_u)
    vt = _pack_bf16(peer_v, transpose=True)
    half = RET_DK // 2
    inv = 1.0 / (ROPE_BASE ** jnp.linspace(0.0, 1.0, half, dtype=F32))
    inv = jnp.repeat(inv, 2).reshape(1, RET_DK)

    h_p, h2t_p, st_p, k_p, v_p = _prompt_mixer(x_prompt, g1, win, sinks, wout, g2, inv, tq=512)
    y_p = _peer_and_final(h_p, h2t_p, wqt, keys, u, vt, fg, tt_route=256, tt=512, et=1024)

    xs = x_sample.reshape(n_s, d)
    proj = _sample_proj(xs, g1, win, inv)
    st_s, k_s, v_s, o2, sres = _sample_state(
        proj, state_ret, cache_swa_k.reshape(n_s, WINDOW, 128),
        cache_swa_v.reshape(n_s, WINDOW, 128), sinks, group=8)
    h_s, h2t_s = _sample_finish(xs, proj, o2, sres.reshape(n_s, SWA_Q_HEADS * 128), wout, g2)
    y_s = _peer_and_final(h_s, h2t_s, wqt, keys, u, vt, fg, tt_route=128, tt=128, et=1024)

    kv_shape = (WINDOW, 2, SWA_HD)
    return (y_p.reshape(batch, seq, d), y_s.reshape(n_s, 1, d), st_p,
            k_p.reshape(batch, *kv_shape), v_p.reshape(batch, *kv_shape),
            st_s, k_s.reshape(n_s, *kv_shape), v_s.reshape(n_s, *kv_shape))
```

```python
import functools
import math

import jax
import jax.numpy as jnp
import numpy as np
from jax import lax
from jax.experimental import pallas as pl
from jax.experimental.pallas import tpu as pltpu

F32 = jnp.float32
BF16 = jnp.bfloat16

D_MODEL = 1024
SEQ = 8192
PAST_LEN = 8192
RET_HEADS = 4
RET_DK = 128
CHUNK = 128
ROPE_BASE = 10000.0
SWA_Q_HEADS = 8
SWA_HD = 64
WINDOW = 128
D_IN = 2816
OFF_RQ, OFF_RK, OFF_RV, OFF_RG, OFF_SQ, OFF_SK, OFF_SV = 0, 512, 1024, 1536, 2048, 2560, 2688
PEER_HEADS = 8
PEER_N_KEYS = 128
PEER_N_EXPERTS = PEER_N_KEYS * PEER_N_KEYS
PEER_TOPK = 16
RMS_EPS = 1e-6
LANES = 128

LOG_DECAY = tuple(math.log(1.0 - 2.0 ** (-5.0 - h)) for h in range(RET_HEADS))

VMEM_LIMIT_BYTES = 56 * 1024 * 1024
MXU_CHUNKS = 4

NEG_INF = float("-inf")
NT_DIMS = (((1,), (1,)), ((), ()))
TN_DIMS = (((0,), (0,)), ((), ()))


def _rms_norm(x, g):
    return x * lax.rsqrt(jnp.mean(x * x, axis=-1, keepdims=True) + RMS_EPS) * g


def _rotation_tables(pos, inv):
    ang = pos * inv
    cos = jnp.cos(ang)
    sin = jnp.sin(ang)
    lane = lax.broadcasted_iota(jnp.int32, ang.shape, 1)
    even = (lane & 1) == 0
    return cos, jnp.where(even, -sin, sin), even


def _rotate_pairs(x, cos, sin_signed, even):
    partner = jnp.where(even, pltpu.roll(x, LANES - 1, 1), pltpu.roll(x, 1, 1))
    return x * cos + partner * sin_signed


def _silu(g):
    return g * jax.nn.sigmoid(g)


def _head_half_variants(a, lo):
    rolled = pltpu.roll(a, 64, 1)
    zero = jnp.zeros_like(a)
    return (jnp.where(lo, a, zero), jnp.where(lo, zero, rolled),
            jnp.where(lo, rolled, zero), jnp.where(lo, zero, a))


def _prompt_mixer_kernel(x_ref, g1_ref, win_ref, sinks_ref, wout_ref, g2_ref, inv_ref,
                         h_ref, h2t_ref, st_ref, kp_ref, vp_ref,
                         proj_ref, state_ref, kcat_ref, vcat_ref, mix_ref,
                         dmask_ref, qdec_ref, kdec_ref, *, tq):
    s = pl.program_id(1)
    n_chunks = tq // CHUNK

    @pl.when(s == 0)
    def _init():
        state_ref[...] = jnp.zeros_like(state_ref)
        kcat_ref[...] = jnp.zeros_like(kcat_ref)
        vcat_ref[...] = jnp.zeros_like(vcat_ref)
        t_row = lax.broadcasted_iota(jnp.int32, (CHUNK, CHUNK), 0).astype(F32)
        t_col = lax.broadcasted_iota(jnp.int32, (CHUNK, CHUNK), 1).astype(F32)
        diff = t_row - t_col
        causal = diff >= 0
        for hh in range(RET_HEADS):
            lg = LOG_DECAY[hh]
            dmask_ref[hh] = jnp.where(causal, jnp.exp(jnp.where(causal, diff, 0.0) * lg), 0.0)
            qdec_ref[hh] = jnp.exp((t_row + 1.0) * lg)
            kdec_ref[hh] = jnp.exp((CHUNK - 1.0 - t_row) * lg)

    x = x_ref[...]
    xn = _rms_norm(x, g1_ref[...]).astype(BF16)
    proj_ref[...] = jnp.dot(xn, pltpu.bitcast(win_ref[...], BF16), preferred_element_type=F32)

    def chunk_body(c, carry):
        r0 = pl.multiple_of(c * CHUNK, CHUNK)
        rows = pl.ds(r0, CHUNK)
        chunk_idx = s * n_chunks + c
        t_local = lax.broadcasted_iota(jnp.int32, (CHUNK, 1), 0)
        pos = (chunk_idx * CHUNK + t_local).astype(F32)
        cos, sin_signed, even = _rotation_tables(pos, inv_ref[...])

        ret_heads = range(RET_HEADS)
        swa_heads = range(SWA_Q_HEADS)

        col = lambda off, hh: slice(off + hh * 128, off + (hh + 1) * 128)
        k_ret = [_rotate_pairs(proj_ref[rows, col(OFF_RK, hh)], cos, sin_signed, even)
                 * (RET_DK ** -0.5) for hh in ret_heads]
        qb = [_rotate_pairs(proj_ref[rows, col(OFF_RQ, hh)], cos, sin_signed, even).astype(BF16)
              for hh in ret_heads]
        kb = [k.astype(BF16) for k in k_ret]
        k_dec = [(k_ret[hh] * kdec_ref[hh]).astype(BF16) for hh in ret_heads]
        vb = [proj_ref[rows, col(OFF_RV, hh)].astype(BF16) for hh in ret_heads]
        kc = proj_ref[rows, OFF_SK:OFF_SK + 128]
        vc = proj_ref[rows, OFF_SV:OFF_SV + 128]
        kp_ref[...] = kc
        vp_ref[...] = vc
        lane = lax.broadcasted_iota(jnp.int32, (CHUNK, LANES), 1)
        lo = lane < 64
        for idx, (kvar, vvar) in enumerate(zip(_head_half_variants(kc, lo),
                                                _head_half_variants(vc, lo))):
            kcat_ref[idx, CHUNK:2 * CHUNK, :] = kvar.astype(BF16)
            vcat_ref[idx, CHUNK:2 * CHUNK, :] = vvar.astype(BF16)
        q_swa = [proj_ref[rows, col(OFF_SQ, i)].astype(BF16) for i in range(SWA_Q_HEADS // 2)]
        kv_idx = lambda head: (head // 4) * 2 + head % 2

        ret_scores = [lax.dot_general(qb[hh], kb[hh], NT_DIMS, preferred_element_type=F32)
                      for hh in ret_heads]
        ret_cross = [jnp.dot(qb[hh], state_ref[hh].astype(BF16), preferred_element_type=F32)
                     for hh in ret_heads]
        ret_kv = [lax.dot_general(k_dec[hh], vb[hh], TN_DIMS, preferred_element_type=F32)
                  for hh in ret_heads]
        swa_scores = [lax.dot_general(q_swa[head // 2], kcat_ref[kv_idx(head)], NT_DIMS,
                                      preferred_element_type=F32) for head in swa_heads]

        for hh in ret_heads:
            state_ref[hh] = state_ref[hh] * math.exp(CHUNK * LOG_DECAY[hh]) + ret_kv[hh]
        ret_p = [(ret_scores[hh] * dmask_ref[hh]).astype(BF16) for hh in ret_heads]
        q_row = lax.broadcasted_iota(jnp.int32, (CHUNK, 2 * CHUNK), 0)
        k_col = lax.broadcasted_iota(jnp.int32, (CHUNK, 2 * CHUNK), 1)
        valid = (k_col >= q_row + 1) & (k_col <= q_row + WINDOW)
        valid = valid & ((k_col >= CHUNK) | (chunk_idx > 0))
        swa_p, swa_den = [], []
        for head in swa_heads:
            sink = sinks_ref[head:head + 1, :]
            sc = jnp.where(valid, swa_scores[head] * (SWA_HD ** -0.5), NEG_INF)
            m = jnp.maximum(jnp.max(sc, axis=-1, keepdims=True), sink)
            p = jnp.exp(sc - m)
            swa_den.append(jnp.sum(p, axis=-1, keepdims=True) + jnp.exp(sink - m))
            swa_p.append(p.astype(BF16))

        ret_o = [jnp.dot(ret_p[hh], vb[hh], preferred_element_type=F32) for hh in ret_heads]
        swa_o = [jnp.dot(swa_p[head], vcat_ref[kv_idx(head)], preferred_element_type=F32)
                 for head in swa_heads]

        for hh in ret_heads:
            o = ret_o[hh] + ret_cross[hh] * qdec_ref[hh]
            o = o * lax.rsqrt(jnp.mean(o * o, axis=-1, keepdims=True) + RMS_EPS)
            gate = proj_ref[rows, col(OFF_RG, hh)]
            mix_ref[rows, col(0, hh)] = (o * _silu(gate)).astype(BF16)
        for i in range(SWA_Q_HEADS // 2):
            pair = swa_o[2 * i] / swa_den[2 * i] + swa_o[2 * i + 1] / swa_den[2 * i + 1]
            mix_ref[rows, col(512, i)] = pair.astype(BF16)
        for idx in range(4):
            kcat_ref[idx, 0:CHUNK, :] = kcat_ref[idx, CHUNK:2 * CHUNK, :]
            vcat_ref[idx, 0:CHUNK, :] = vcat_ref[idx, CHUNK:2 * CHUNK, :]
        return carry

    lax.fori_loop(0, n_chunks, chunk_body, 0)

    h = x + jnp.dot(mix_ref[...], pltpu.bitcast(wout_ref[...], BF16),
                             preferred_element_type=F32)
    h_ref[...] = h
    h2t_ref[...] = pltpu.bitcast(_rms_norm(h, g2_ref[...]).T.astype(BF16), jnp.uint32)
    st_ref[...] = state_ref[...]


def _prompt_mixer(x, g1, win, sinks, wout, g2, inv, *, tq):
    batch, seq, d = x.shape
    ns = seq // tq
    full = lambda shape: pl.BlockSpec(shape, lambda b, s: (0,) * len(shape))
    return pl.pallas_call(
        functools.partial(_prompt_mixer_kernel, tq=tq),
        grid=(batch, ns),
        in_specs=[
            pl.BlockSpec((None, tq, d), lambda b, s: (b, s, 0)),
            full((1, d)), full((d // 2, D_IN)), full((SWA_Q_HEADS, 1)), full((d // 2, d)),
            full((1, d)),
            full((1, LANES)),
        ],
        out_specs=[
            pl.BlockSpec((tq, d), lambda b, s: (b * ns + s, 0)),
            pl.BlockSpec((d // 2, tq), lambda b, s: (0, b * ns + s)),
            pl.BlockSpec((None, RET_HEADS, 128, 128), lambda b, s: (b, 0, 0, 0)),
            pl.BlockSpec((None, WINDOW, 128), lambda b, s: (b, 0, 0)),
            pl.BlockSpec((None, WINDOW, 128), lambda b, s: (b, 0, 0)),
        ],
        out_shape=[
            jax.ShapeDtypeStruct((batch * seq, d), F32),
            jax.ShapeDtypeStruct((d // 2, batch * seq), jnp.uint32),
            jax.ShapeDtypeStruct((batch, RET_HEADS, 128, 128), F32),
            jax.ShapeDtypeStruct((batch, WINDOW, 128), F32),
            jax.ShapeDtypeStruct((batch, WINDOW, 128), F32),
        ],
        scratch_shapes=[
            pltpu.VMEM((tq, D_IN), F32),
            pltpu.VMEM((RET_HEADS, 128, 128), F32),
            pltpu.VMEM((4, 2 * CHUNK, LANES), BF16),
            pltpu.VMEM((4, 2 * CHUNK, LANES), BF16),
            pltpu.VMEM((tq, d), BF16),
            pltpu.VMEM((RET_HEADS, CHUNK, CHUNK), F32),
            pltpu.VMEM((RET_HEADS, CHUNK, CHUNK), F32),
            pltpu.VMEM((RET_HEADS, CHUNK, CHUNK), F32),
        ],
        compiler_params=pltpu.CompilerParams(
            dimension_semantics=("arbitrary", "arbitrary"),
            vmem_limit_bytes=VMEM_LIMIT_BYTES),
        name="prompt_mixer",
    )(x, g1, win, sinks, wout, g2, inv)


def _sample_proj_kernel(x_ref, g1_ref, win_ref, inv_ref, proj_ref):
    xn = _rms_norm(x_ref[...], g1_ref[...]).astype(BF16)
    proj = jnp.dot(xn, pltpu.bitcast(win_ref[...], BF16), preferred_element_type=F32)
    proj_ref[...] = proj
    pos = jnp.full((1, 1), float(PAST_LEN), F32)
    cos, sin_signed, even = _rotation_tables(pos * jnp.ones((1, LANES), F32), inv_ref[...])
    even_full = jnp.broadcast_to(even, (x_ref.shape[0], LANES))
    for hh in range(RET_HEADS):
        ql = slice(OFF_RQ + hh * 128, OFF_RQ + (hh + 1) * 128)
        kl = slice(OFF_RK + hh * 128, OFF_RK + (hh + 1) * 128)
        proj_ref[:, ql] = _rotate_pairs(proj[:, ql], cos, sin_signed, even_full)
        proj_ref[:, kl] = _rotate_pairs(proj[:, kl], cos, sin_signed, even_full) * (RET_DK ** -0.5)


def _sample_proj(x, g1, win, inv):
    n = x.shape[0]
    return pl.pallas_call(
        _sample_proj_kernel,
        out_shape=jax.ShapeDtypeStruct((n, D_IN), F32),
        compiler_params=pltpu.CompilerParams(vmem_limit_bytes=VMEM_LIMIT_BYTES),
        name="sample_proj",
    )(x, g1, win, inv)


def _sample_state_kernel(proj_ref, st_ref, ck_ref, cv_ref, sinks_ref,
                         stn_ref, kn_ref, vn_ref, o2_ref, sres_ref, *, group):
    row8 = lax.broadcasted_iota(jnp.int32, (8, LANES), 0)
    lane8 = lax.broadcasted_iota(jnp.int32, (8, LANES), 1)
    lo1 = lax.broadcasted_iota(jnp.int32, (1, LANES), 1) < 64
    sink_col = sinks_ref[...]

    def sample_body(b, carry):
        row = pl.ds(b, 1)
        for hh in range(RET_HEADS):
            gamma = math.exp(LOG_DECAY[hh])
            q = proj_ref[row, OFF_RQ + hh * 128:OFF_RQ + (hh + 1) * 128]
            k = proj_ref[row, OFF_RK + hh * 128:OFF_RK + (hh + 1) * 128]
            v = proj_ref[row, OFF_RV + hh * 128:OFF_RV + (hh + 1) * 128]
            state = st_ref[b, hh]
            q8 = jnp.broadcast_to(q, (8, LANES)).astype(BF16)
            o2 = jnp.dot(q8, state.astype(BF16), preferred_element_type=F32)
            o2_ref[row, hh * 128:(hh + 1) * 128] = o2[0:1, :]
            k8 = jnp.where(row8 == 0, jnp.broadcast_to(k, (8, LANES)), 0.0).astype(BF16)
            v8 = jnp.broadcast_to(v, (8, LANES)).astype(BF16)
            outer = lax.dot_general(k8, v8, TN_DIMS, preferred_element_type=F32)
            stn_ref[b, hh] = state * gamma + outer
        kn_ref[b, 0:WINDOW - 1, :] = ck_ref[b, 1:WINDOW, :]
        vn_ref[b, 0:WINDOW - 1, :] = cv_ref[b, 1:WINDOW, :]
        kn_ref[b, WINDOW - 1:WINDOW, :] = proj_ref[row, OFF_SK:OFF_SK + 128]
        vn_ref[b, WINDOW - 1:WINDOW, :] = proj_ref[row, OFF_SV:OFF_SV + 128]
        kwin = kn_ref[b].astype(BF16)
        vwin = vn_ref[b].astype(BF16)
        q_pad = jnp.zeros((8, LANES), F32)
        for qb_idx in range(SWA_Q_HEADS // 2):
            g = qb_idx // 2
            blk = proj_ref[row, OFF_SQ + qb_idx * 128:OFF_SQ + (qb_idx + 1) * 128]
            rolled = pltpu.roll(blk, 64, 1)
            zero = jnp.zeros_like(blk)
            if g == 0:
                first, second = jnp.where(lo1, blk, zero), jnp.where(lo1, rolled, zero)
            else:
                first, second = jnp.where(lo1, zero, rolled), jnp.where(lo1, zero, blk)
            q_pad = jnp.where(row8 == 2 * qb_idx, jnp.broadcast_to(first, (8, LANES)), q_pad)
            q_pad = jnp.where(row8 == 2 * qb_idx + 1, jnp.broadcast_to(second, (8, LANES)), q_pad)
        sc = lax.dot_general(q_pad.astype(BF16), kwin, NT_DIMS,
                             preferred_element_type=F32) * (SWA_HD ** -0.5)
        m = jnp.maximum(jnp.max(sc, axis=-1, keepdims=True), sink_col)
        p = jnp.exp(sc - m)
        den = jnp.sum(p, axis=-1, keepdims=True) + jnp.exp(sink_col - m)
        sres_ref[b] = jnp.dot(p.astype(BF16), vwin, preferred_element_type=F32) / den
        return carry

    del lane8
    for b in range(group):
        sample_body(b, 0)


def _sample_state(proj, state, ck, cv, sinks, *, group):
    n = proj.shape[0]
    return pl.pallas_call(
        functools.partial(_sample_state_kernel, group=group),
        grid=(n // group,),
        in_specs=[
            pl.BlockSpec((group, D_IN), lambda i: (i, 0)),
            pl.BlockSpec((group, RET_HEADS, 128, 128), lambda i: (i, 0, 0, 0)),
            pl.BlockSpec((group, WINDOW, 128), lambda i: (i, 0, 0)),
            pl.BlockSpec((group, WINDOW, 128), lambda i: (i, 0, 0)),
            pl.BlockSpec((SWA_Q_HEADS, 1), lambda i: (0, 0)),
        ],
        out_specs=[
            pl.BlockSpec((group, RET_HEADS, 128, 128), lambda i: (i, 0, 0, 0)),
            pl.BlockSpec((group, WINDOW, 128), lambda i: (i, 0, 0)),
            pl.BlockSpec((group, WINDOW, 128), lambda i: (i, 0, 0)),
            pl.BlockSpec((group, 512), lambda i: (i, 0)),
            pl.BlockSpec((group, SWA_Q_HEADS, 128), lambda i: (i, 0, 0)),
        ],
        out_shape=[
            jax.ShapeDtypeStruct((n, RET_HEADS, 128, 128), F32),
            jax.ShapeDtypeStruct((n, WINDOW, 128), F32),
            jax.ShapeDtypeStruct((n, WINDOW, 128), F32),
            jax.ShapeDtypeStruct((n, 512), F32),
            jax.ShapeDtypeStruct((n, SWA_Q_HEADS, 128), F32),
        ],
        compiler_params=pltpu.CompilerParams(
            dimension_semantics=("arbitrary",), vmem_limit_bytes=VMEM_LIMIT_BYTES),
        name="sample_state",
    )(proj, state, ck, cv, sinks)


def _sample_finish_kernel(x_ref, proj_ref, o2_ref, sres_ref, wout_ref, g2_ref,
                          h_ref, h2t_ref, mix_ref):
    n = x_ref.shape[0]
    for hh in range(RET_HEADS):
        gamma = math.exp(LOG_DECAY[hh])
        lanes = slice(hh * 128, (hh + 1) * 128)
        q = proj_ref[:, OFF_RQ + hh * 128:OFF_RQ + (hh + 1) * 128]
        k = proj_ref[:, OFF_RK + hh * 128:OFF_RK + (hh + 1) * 128]
        v = proj_ref[:, OFF_RV + hh * 128:OFF_RV + (hh + 1) * 128]
        gate = proj_ref[:, OFF_RG + hh * 128:OFF_RG + (hh + 1) * 128]
        qk = jnp.sum(q * k, axis=-1, keepdims=True)
        o = qk * v + o2_ref[:, lanes] * gamma
        o = o * lax.rsqrt(jnp.mean(o * o, axis=-1, keepdims=True) + RMS_EPS)
        mix_ref[:, lanes] = (o * _silu(gate)).astype(BF16)
    lo = lax.broadcasted_iota(jnp.int32, (n, LANES), 1) < 64
    for qb_idx in range(SWA_Q_HEADS // 2):
        r0 = sres_ref[:, (2 * qb_idx) * 128:(2 * qb_idx + 1) * 128]
        r1 = sres_ref[:, (2 * qb_idx + 1) * 128:(2 * qb_idx + 2) * 128]
        if qb_idx // 2 == 0:
            blk = jnp.where(lo, r0, pltpu.roll(r1, 64, 1))
        else:
            blk = jnp.where(lo, pltpu.roll(r0, 64, 1), r1)
        mix_ref[:, 512 + qb_idx * 128:512 + (qb_idx + 1) * 128] = blk.astype(BF16)
    h = x_ref[...] + jnp.dot(mix_ref[...], pltpu.bitcast(wout_ref[...], BF16),
                             preferred_element_type=F32)
    h_ref[...] = h
    h2t_ref[...] = pltpu.bitcast(_rms_norm(h, g2_ref[...]).T.astype(BF16), jnp.uint32)


def _sample_finish(x, proj, o2, sres, wout, g2):
    n, d = x.shape
    return pl.pallas_call(
        _sample_finish_kernel,
        out_shape=[jax.ShapeDtypeStruct((n, d), F32),
                   jax.ShapeDtypeStruct((d // 2, n), jnp.uint32)],
        scratch_shapes=[pltpu.VMEM((n, d), BF16)],
        compiler_params=pltpu.CompilerParams(vmem_limit_bytes=VMEM_LIMIT_BYTES),
        name="sample_finish",
    )(x, proj, o2, sres, wout, g2)


def _sorting_network(n):
    def merge(lo, hi, r):
        step = r * 2
        if step < hi - lo:
            yield from merge(lo, hi, step)
            yield from merge(lo + r, hi, step)
            yield from [(i, i + r) for i in range(lo + r, hi - r, step)]
        else:
            yield (lo, lo + r)

    def sort(lo, hi):
        if hi - lo >= 1:
            mid = lo + (hi - lo) // 2
            yield from sort(lo, mid)
            yield from sort(mid + 1, hi)
            yield from merge(lo, hi, 1)

    return tuple(sort(0, n - 1))


def _top16_values(s):
    n_lists = s.shape[0] // 8
    lists = [s[k * 8:(k + 1) * 8, :] for k in range(n_lists)]
    for i, j in _sorting_network(n_lists):
        lists[i], lists[j] = jnp.maximum(lists[i], lists[j]), jnp.minimum(lists[i], lists[j])
    rows = []
    for p in range(PEER_TOPK):
        m = jnp.max(lists[0], axis=0, keepdims=True)
        rows.append(m)
        if p + 1 < PEER_TOPK:
            hit = lists[0] == m
            for k in range(PEER_TOPK - 1 - p):
                lists[k] = jnp.where(hit, lists[k + 1], lists[k])
    return rows


def _top16_ranked(s):
    work = s
    rows = []
    rank = jnp.full(s.shape, float(PEER_TOPK), F32)
    for p in range(PEER_TOPK):
        m = jnp.max(work, axis=0, keepdims=True)
        rows.append(m)
        hit = work == m
        rank = jnp.where(hit, float(p), rank)
        if p + 1 < PEER_TOPK:
            work = jnp.where(hit, NEG_INF, work)
    return rows, rank


def _stack8(rows, row8):
    out = jnp.zeros(row8.shape, F32)
    for r, v in enumerate(rows):
        out = jnp.where(row8 == r, jnp.broadcast_to(v, row8.shape), out)
    return out


def _routing_tables(s0, s1):
    a = _top16_values(s0)
    b, rank1 = _top16_ranked(s1)
    row8 = lax.broadcasted_iota(jnp.int32, (8, s0.shape[1]), 0)
    a_lo, a_hi = _stack8(a[:8], row8), _stack8(a[8:], row8)
    b_lo, b_hi = _stack8(b[:8], row8), _stack8(b[8:], row8)
    neg = jnp.full(row8.shape, NEG_INF, F32)
    cands = [
        a[0] + b_lo, a[0] + b_hi, a[1] + b_lo,
        jnp.where(row8 < 5, a[2] + b_lo, neg),
        jnp.where(row8 < 4, a[3] + b_lo, neg),
        b[0] + a_hi,
        jnp.where(row8 >= 4, b[0] + a_lo, neg),
        jnp.where(row8 >= 4, b[1] + a_lo, neg),
        jnp.where(row8 == 4, b[2] + a_lo, neg),
    ]
    work = cands
    tau = None
    for it in range(PEER_TOPK):
        m = functools.reduce(jnp.maximum, work)
        tau = jnp.max(m, axis=0, keepdims=True)
        if it + 1 < PEER_TOPK:
            work = [jnp.where(w == tau, NEG_INF, w) for w in work]
    top = a[0] + b[0]
    passed = [c >= tau for c in cands]
    col_sum = lambda x: jnp.sum(x, axis=0, keepdims=True)
    z = functools.reduce(lambda x, y: x + y, [
        col_sum(jnp.where(ok, jnp.exp(c - top), 0.0)) for ok, c in zip(passed, cands)])
    cnt = [jnp.where(ok, 1.0, 0.0) for ok in passed]
    by_row = cnt[6] + cnt[7] + cnt[8]
    n = [col_sum(cnt[0]) + col_sum(cnt[1]), col_sum(cnt[2]), col_sum(cnt[3]), col_sum(cnt[4])]
    n += [by_row[p:p + 1, :] for p in range(4, 8)]
    n += [cnt[5][p:p + 1, :] for p in range(8)]
    n0 = jnp.zeros(s0.shape, F32)
    for p in range(PEER_TOPK):
        n0 = jnp.where(s0 == a[p], n[p], n0)
    e1 = jnp.exp(s1 - b[0])
    e0n = jnp.exp(s0 - a[0]) * (1.0 / z)
    return rank1, e1, n0, e0n


def _routing_kernel(h2t_ref, wqt_ref, keys_ref, r1_ref, e1_ref, n0_ref, e0_ref,
                    q_ref, s_ref, *, tt):
    q_ref[...] = jnp.dot(pltpu.bitcast(wqt_ref[...], BF16), pltpu.bitcast(h2t_ref[...], BF16),
                         preferred_element_type=F32)
    n_lg = tt // LANES

    def head_body(h, carry):
        for c in range(2):
            r0 = pl.multiple_of((2 * h + c) * 128, 128)
            q_hc = q_ref[pl.ds(r0, 128), :].astype(BF16)
            key_words = keys_ref[pl.ds(pl.multiple_of((2 * h + c) * 64, 64), 64), :]
            s_ref[c] = jnp.dot(pltpu.bitcast(key_words, BF16), q_hc, preferred_element_type=F32)
        out_rows = pl.ds(pl.multiple_of(h * 128, 128), 128)
        packed_rows = pl.ds(pl.multiple_of(h * 64, 64), 64)
        rank1, e1, n0, e0n = _routing_tables(s_ref[0], s_ref[1])
        for lg in range(n_lg):
            lanes = slice(lg * LANES, (lg + 1) * LANES)
            r1_ref[lg, packed_rows, :] = pltpu.bitcast(rank1[:, lanes].astype(BF16), jnp.uint32)
            e1_ref[lg, packed_rows, :] = pltpu.bitcast(e1[:, lanes].astype(BF16), jnp.uint32)
            n0_ref[lg, out_rows, :] = n0[:, lanes]
            e0_ref[lg, out_rows, :] = e0n[:, lanes]
        return carry

    lax.fori_loop(0, PEER_HEADS, head_body, 0)


def _routing(h2t, wqt, keys, *, tt):
    d, t = 2 * h2t.shape[0], h2t.shape[1]
    rows = PEER_HEADS * PEER_N_KEYS
    tab = lambda r, dt: jax.ShapeDtypeStruct((t // LANES, r, LANES), dt)
    tab_spec = lambda r: pl.BlockSpec((tt // LANES, r, LANES), lambda i: (i, 0, 0))
    return pl.pallas_call(
        functools.partial(_routing_kernel, tt=tt),
        grid=(t // tt,),
        in_specs=[
            pl.BlockSpec((d // 2, tt), lambda i: (0, i)),
            pl.BlockSpec(wqt.shape, lambda i: (0, 0)),
            pl.BlockSpec(keys.shape, lambda i: (0, 0)),
        ],
        out_specs=[tab_spec(rows // 2), tab_spec(rows // 2), tab_spec(rows), tab_spec(rows)],
        out_shape=[tab(rows // 2, jnp.uint32), tab(rows // 2, jnp.uint32),
                   tab(rows, F32), tab(rows, F32)],
        scratch_shapes=[pltpu.VMEM((2 * wqt.shape[0], tt), F32), pltpu.VMEM((2, 128, tt), F32)],
        compiler_params=pltpu.CompilerParams(
            dimension_semantics=("arbitrary",), vmem_limit_bytes=VMEM_LIMIT_BYTES),
        name="peer_routing",
    )(h2t, wqt, keys)


def _experts_kernel(xt_ref, u_ref, vt_ref, r1_ref, e1_ref, n0_ref, e0_ref, h_ref, fg_ref,
                    y_ref, acc_ref, hd_ref, at_ref, *, tt, et, n_e, n_tiles):
    g = pl.program_id(0)
    e_c = jnp.clip(g - 2, 0, n_tiles - 1) % n_e

    @pl.when(g == 0)
    def _prime():
        hd_ref[...] = jnp.zeros_like(hd_ref)
        at_ref[...] = jnp.zeros_like(at_ref)

    @pl.when(e_c == 0)
    def _new_token_tile():
        acc_ref[...] = jnp.zeros_like(acc_ref)

    stages = functools.partial(_experts_stages, xt_ref, u_ref, vt_ref, r1_ref, e1_ref, n0_ref,
                               e0_ref, acc_ref, hd_ref, at_ref, tt=tt, et=et, n_e=n_e,
                               n_tiles=n_tiles)
    pl.when(g % 2 == 0)(functools.partial(stages, slot_a=0))
    pl.when(g % 2 == 1)(functools.partial(stages, slot_a=1))

    @pl.when((g >= 2) & (e_c == n_e - 1))
    def _finish():
        y_ref[...] = _rms_norm(h_ref[...] + acc_ref[...].T, fg_ref[...])


def _experts_stages(xt_ref, u_ref, vt_ref, r1_ref, e1_ref, n0_ref, e0_ref, acc_ref, hd_ref,
                    at_ref, *, tt, et, n_e, n_tiles, slot_a):
    slot_b = 1 - slot_a
    n_ib = et // PEER_N_KEYS
    sub = 16
    e_b = jnp.clip(pl.program_id(0) - 1, 0, n_tiles - 1) % n_e

    def packed(rows):
        return slice(rows.start // 2, rows.stop // 2)

    def stage_a(rows):
        u = pltpu.bitcast(u_ref[packed(rows), :], BF16)
        xt = pltpu.bitcast(xt_ref[...], BF16)
        hd_ref[slot_a, rows, :] = jnp.dot(u, xt, preferred_element_type=F32)

    def stage_b(ii, lg):
        lanes = slice(lg * LANES, (lg + 1) * LANES)
        rows = slice(ii * 128, (ii + 1) * 128)
        w = jnp.zeros((PEER_N_KEYS, LANES), BF16)
        for h in range(PEER_HEADS):
            grp = pl.ds(pl.multiple_of(h * 128 + e_b * n_ib + (ii // 8) * 8, 8), 8)
            n0b = jnp.broadcast_to(n0_ref[lg, grp, :][ii % 8:ii % 8 + 1, :], (sub, LANES))
            e0b = jnp.broadcast_to(e0_ref[lg, grp, :][ii % 8:ii % 8 + 1, :], (sub, LANES))
            n0b = jnp.concatenate([n0b.astype(BF16)] * (PEER_N_KEYS // sub), axis=0)
            e0b = jnp.concatenate([e0b.astype(BF16)] * (PEER_N_KEYS // sub), axis=0)
            r1 = pltpu.bitcast(r1_ref[lg, h * 64:(h + 1) * 64, :], BF16)
            e1 = pltpu.bitcast(e1_ref[lg, h * 64:(h + 1) * 64, :], BF16)
            w = w + jnp.where(r1 < n0b, e1 * e0b, jnp.zeros_like(e1))
        x = hd_ref[slot_b, rows, lanes]
        gelu = 0.5 * x * (1.0 + lax.erf(x * math.sqrt(0.5)))
        at_ref[slot_b, rows, lanes] = gelu.astype(BF16) * w

    def stage_c(rows):
        vt = pltpu.bitcast(vt_ref[packed(rows), :], BF16)
        acc_ref[rows, :] += jnp.dot(vt, at_ref[slot_a], preferred_element_type=F32)

    d = acc_ref.shape[0]
    n_lg = tt // LANES
    b_pieces = [(ii, lg) for ii in range(n_ib) for lg in range(n_lg)]
    per_chunk = len(b_pieces) // MXU_CHUNKS
    quarter = max(per_chunk // 4, 1)
    for k in range(MXU_CHUNKS):
        mine = b_pieces[k * per_chunk:(k + 1) * per_chunk]
        for piece in mine[:quarter]:
            stage_b(*piece)
        stage_a(slice(k * et // MXU_CHUNKS, (k + 1) * et // MXU_CHUNKS))
        for piece in mine[quarter:3 * quarter]:
            stage_b(*piece)
        stage_c(slice(k * d // MXU_CHUNKS, (k + 1) * d // MXU_CHUNKS))
        for piece in mine[3 * quarter:]:
            stage_b(*piece)


def _experts(xt, u, vt, r1, e1, n0, e0, h, fg, *, tt, et):
    d, t = 2 * xt.shape[0], xt.shape[1]
    n_exp = 2 * u.shape[0]
    rows = n0.shape[1]
    assert et % (8 * PEER_N_KEYS) == 0 and t % tt == 0 and n_exp % et == 0
    n_e = n_exp // et
    n_tiles = (t // tt) * n_e
    tile_a = lambda g: jnp.minimum(g, n_tiles - 1)
    tile_b = lambda g: jnp.clip(g - 1, 0, n_tiles - 1)
    tile_c = lambda g: jnp.clip(g - 2, 0, n_tiles - 1)
    tab_spec = lambda r: pl.BlockSpec((tt // LANES, r, LANES), lambda g: (tile_b(g) // n_e, 0, 0))
    return pl.pallas_call(
        functools.partial(_experts_kernel, tt=tt, et=et, n_e=n_e, n_tiles=n_tiles),
        grid=(n_tiles + 2,),
        in_specs=[
            pl.BlockSpec((d // 2, tt), lambda g: (0, tile_a(g) // n_e)),
            pl.BlockSpec((et // 2, d), lambda g: (tile_a(g) % n_e, 0)),
            pl.BlockSpec((d // 2, et), lambda g: (0, tile_c(g) % n_e)),
            tab_spec(rows // 2), tab_spec(rows // 2), tab_spec(rows), tab_spec(rows),
            pl.BlockSpec((tt, d), lambda g: (tile_c(g) // n_e, 0)),
            pl.BlockSpec((1, d), lambda g: (0, 0)),
        ],
        out_specs=pl.BlockSpec((tt, d), lambda g: (tile_c(g) // n_e, 0)),
        out_shape=jax.ShapeDtypeStruct((t, d), F32),
        scratch_shapes=[
            pltpu.VMEM((d, tt), F32),
            pltpu.VMEM((2, et, tt), F32),
            pltpu.VMEM((2, et, tt), BF16),
        ],
        compiler_params=pltpu.CompilerParams(
            dimension_semantics=("arbitrary",),
            vmem_limit_bytes=VMEM_LIMIT_BYTES),
        name="peer_experts",
    )(xt, u, vt, r1, e1, n0, e0, h, fg)


def _pack_kernel(x_ref, o_ref, *, transpose):
    x = x_ref[...]
    if transpose:
        x = x.T
    o_ref[...] = pltpu.bitcast(x.astype(BF16), jnp.uint32)


def _pack_bf16(x, *, transpose=False, block_rows=1024):
    m, n = x.shape
    bm = min(block_rows, m)
    assert m % bm == 0
    if transpose:
        out_shape, out_spec = (n // 2, m), pl.BlockSpec((n // 2, bm), lambda i: (0, i))
    else:
        out_shape, out_spec = (m // 2, n), pl.BlockSpec((bm // 2, n), lambda i: (i, 0))
    return pl.pallas_call(
        functools.partial(_pack_kernel, transpose=transpose),
        grid=(m // bm,),
        in_specs=[pl.BlockSpec((bm, n), lambda i: (i, 0))],
        out_specs=out_spec,
        out_shape=jax.ShapeDtypeStruct(out_shape, jnp.uint32),
        compiler_params=pltpu.CompilerParams(
            dimension_semantics=("arbitrary",), vmem_limit_bytes=VMEM_LIMIT_BYTES),
        name="pack_bf16",
    )(x)


def _peer_and_final(h, h2t, wqt, keys, u, vt, fg, *, tt_route, tt, et):
    r1, e1, n0, e0 = _routing(h2t, wqt, keys, tt=tt_route)
    return _experts(h2t, u, vt, r1, e1, n0, e0, h, fg, tt=tt, et=et)


def kernel(x_prompt, x_sample, state_ret, cache_swa_k, cache_swa_v, norm1_g, w_in, swa_sinks, w_out, norm2_g, peer_w_q, peer_sub_keys, peer_u, peer_v, final_g):
    batch, seq, d = x_prompt.shape
    n_s = x_sample.shape[0]
    g1 = norm1_g.reshape(1, d)
    g2 = norm2_g.reshape(1, d)
    fg = final_g.reshape(1, d)
    sinks = swa_sinks.astype(F32).reshape(SWA_Q_HEADS, 1)
    win = _pack_bf16(w_in)
    wout = _pack_bf16(w_out)
    wqt = _pack_bf16(peer_w_q, transpose=True)
    keys = _pack_bf16(peer_sub_keys.reshape(PEER_HEADS * 2 * PEER_N_KEYS, -1))
    u = _pack_bf16(peer_u)
    vt = _pack_bf16(peer_v, transpose=True)
    half = RET_DK // 2
    inv = 1.0 / (ROPE_BASE ** jnp.linspace(0.0, 1.0, half, dtype=F32))
    inv = jnp.repeat(inv, 2).reshape(1, RET_DK)

    h_p, h2t_p, st_p, k_p, v_p = _prompt_mixer(x_prompt, g1, win, sinks, wout, g2, inv, tq=512)
    y_p = _peer_and_final(h_p, h2t_p, wqt, keys, u, vt, fg, tt_route=256, tt=512, et=1024)

    xs = x_sample.reshape(n_s, d)
    proj = _sample_proj(xs, g1, win, inv)
    st_s, k_s, v_s, o2, sres = _sample_state(
        proj, state_ret, cache_swa_k.reshape(n_s, WINDOW, 128),
        cache_swa_v.reshape(n_s, WINDOW, 128), sinks, group=8)
    h_s, h2t_s = _sample_finish(xs, proj, o2, sres.reshape(n_s, SWA_Q_HEADS * 128), wout, g2)
    y_s = _peer_and_final(h_s, h2t_s, wqt, keys, u, vt, fg, tt_route=128, tt=128, et=1024)

    kv_shape = (WINDOW, 2, SWA_HD)
    return (y_p.reshape(batch, seq, d), y_s.reshape(n_s, 1, d), st_p,
            k_p.reshape(batch, *kv_shape), v_p.reshape(batch, *kv_shape),
            st_s, k_s.reshape(n_s, *kv_shape), v_s.reshape(n_s, *kv_shape))
```

```python
import functools
import math

import jax
import jax.numpy as jnp
import numpy as np
from jax import lax
from jax.experimental import pallas as pl
from jax.experimental.pallas import tpu as pltpu

F32 = jnp.float32
BF16 = jnp.bfloat16

D_MODEL = 1024
SEQ = 8192
PAST_LEN = 8192
RET_HEADS = 4
RET_DK = 128
CHUNK = 128
ROPE_BASE = 10000.0
SWA_Q_HEADS = 8
SWA_HD = 64
WINDOW = 128
D_IN = 2816
OFF_RQ, OFF_RK, OFF_RV, OFF_RG, OFF_SQ, OFF_SK, OFF_SV = 0, 512, 1024, 1536, 2048, 2560, 2688
PEER_HEADS = 8
PEER_N_KEYS = 128
PEER_N_EXPERTS = PEER_N_KEYS * PEER_N_KEYS
PEER_TOPK = 16
RMS_EPS = 1e-6
LANES = 128

LOG_DECAY = tuple(math.log(1.0 - 2.0 ** (-5.0 - h)) for h in range(RET_HEADS))

VMEM_LIMIT_BYTES = 56 * 1024 * 1024
MXU_CHUNKS = 4

NEG_INF = float("-inf")
NT_DIMS = (((1,), (1,)), ((), ()))
TN_DIMS = (((0,), (0,)), ((), ()))


def _rms_norm(x, g):
    return x * lax.rsqrt(jnp.mean(x * x, axis=-1, keepdims=True) + RMS_EPS) * g


def _rotation_tables(pos, inv):
    ang = pos * inv
    cos = jnp.cos(ang)
    sin = jnp.sin(ang)
    lane = lax.broadcasted_iota(jnp.int32, ang.shape, 1)
    even = (lane & 1) == 0
    return cos, jnp.where(even, -sin, sin), even


def _rotate_pairs(x, cos, sin_signed, even):
    partner = jnp.where(even, pltpu.roll(x, LANES - 1, 1), pltpu.roll(x, 1, 1))
    return x * cos + partner * sin_signed


def _silu(g):
    return g * jax.nn.sigmoid(g)


def _head_half_variants(a, lo):
    rolled = pltpu.roll(a, 64, 1)
    zero = jnp.zeros_like(a)
    return (jnp.where(lo, a, zero), jnp.where(lo, zero, rolled),
            jnp.where(lo, rolled, zero), jnp.where(lo, zero, a))


def _prompt_mixer_kernel(x_ref, g1_ref, win_ref, sinks_ref, wout_ref, g2_ref, inv_ref,
                         h_ref, h2t_ref, st_ref, kp_ref, vp_ref,
                         proj_ref, state_ref, kcat_ref, vcat_ref, mix_ref,
                         dmask_ref, qdec_ref, kdec_ref, *, tq):
    s = pl.program_id(1)
    n_chunks = tq // CHUNK

    @pl.when(s == 0)
    def _init():
        state_ref[...] = jnp.zeros_like(state_ref)
        kcat_ref[...] = jnp.zeros_like(kcat_ref)
        vcat_ref[...] = jnp.zeros_like(vcat_ref)
        t_row = lax.broadcasted_iota(jnp.int32, (CHUNK, CHUNK), 0).astype(F32)
        t_col = lax.broadcasted_iota(jnp.int32, (CHUNK, CHUNK), 1).astype(F32)
        diff = t_row - t_col
        causal = diff >= 0
        for hh in range(RET_HEADS):
            lg = LOG_DECAY[hh]
            dmask_ref[hh] = jnp.where(causal, jnp.exp(jnp.where(causal, diff, 0.0) * lg), 0.0)
            qdec_ref[hh] = jnp.exp((t_row + 1.0) * lg)
            kdec_ref[hh] = jnp.exp((CHUNK - 1.0 - t_row) * lg)

    x = x_ref[...]
    xn = _rms_norm(x, g1_ref[...]).astype(BF16)
    proj_ref[...] = jnp.dot(xn, pltpu.bitcast(win_ref[...], BF16), preferred_element_type=F32)

    def chunk_body(c, carry):
        r0 = pl.multiple_of(c * CHUNK, CHUNK)
        rows = pl.ds(r0, CHUNK)
        chunk_idx = s * n_chunks + c
        t_local = lax.broadcasted_iota(jnp.int32, (CHUNK, 1), 0)
        pos = (chunk_idx * CHUNK + t_local).astype(F32)
        cos, sin_signed, even = _rotation_tables(pos, inv_ref[...])

        ret_heads = range(RET_HEADS)
        swa_heads = range(SWA_Q_HEADS)

        col = lambda off, hh: slice(off + hh * 128, off + (hh + 1) * 128)
        k_ret = [_rotate_pairs(proj_ref[rows, col(OFF_RK, hh)], cos, sin_signed, even)
                 * (RET_DK ** -0.5) for hh in ret_heads]
        qb = [_rotate_pairs(proj_ref[rows, col(OFF_RQ, hh)], cos, sin_signed, even).astype(BF16)
              for hh in ret_heads]
        kb = [k.astype(BF16) for k in k_ret]
        k_dec = [(k_ret[hh] * kdec_ref[hh]).astype(BF16) for hh in ret_heads]
        vb = [proj_ref[rows, col(OFF_RV, hh)].astype(BF16) for hh in ret_heads]
        kc = proj_ref[rows, OFF_SK:OFF_SK + 128]
        vc = proj_ref[rows, OFF_SV:OFF_SV + 128]
        kp_ref[...] = kc
        vp_ref[...] = vc
        lane = lax.broadcasted_iota(jnp.int32, (CHUNK, LANES), 1)
        lo = lane < 64
        for idx, (kvar, vvar) in enumerate(zip(_head_half_variants(kc, lo),
                                                _head_half_variants(vc, lo))):
            kcat_ref[idx, CHUNK:2 * CHUNK, :] = kvar.astype(BF16)
            vcat_ref[idx, CHUNK:2 * CHUNK, :] = vvar.astype(BF16)
        q_swa = [proj_ref[rows, col(OFF_SQ, i)].astype(BF16) for i in range(SWA_Q_HEADS // 2)]
        kv_idx = lambda head: (head // 4) * 2 + head % 2

        ret_scores = [lax.dot_general(qb[hh], kb[hh], NT_DIMS, preferred_element_type=F32)
                      for hh in ret_heads]
        ret_cross = [jnp.dot(qb[hh], state_ref[hh].astype(BF16), preferred_element_type=F32)
                     for hh in ret_heads]
        ret_kv = [lax.dot_general(k_dec[hh], vb[hh], TN_DIMS, preferred_element_type=F32)
                  for hh in ret_heads]
        swa_scores = [lax.dot_general(q_swa[head // 2], kcat_ref[kv_idx(head)], NT_DIMS,
                                      preferred_element_type=F32) for head in swa_heads]

        for hh in ret_heads:
            state_ref[hh] = state_ref[hh] * math.exp(CHUNK * LOG_DECAY[hh]) + ret_kv[hh]
        ret_p = [(ret_scores[hh] * dmask_ref[hh]).astype(BF16) for hh in ret_heads]
        q_row = lax.broadcasted_iota(jnp.int32, (CHUNK, 2 * CHUNK), 0)
        k_col = lax.broadcasted_iota(jnp.int32, (CHUNK, 2 * CHUNK), 1)
        valid = (k_col >= q_row + 1) & (k_col <= q_row + WINDOW)
        valid = valid & ((k_col >= CHUNK) | (chunk_idx > 0))
        swa_p, swa_den = [], []
        for head in swa_heads:
            sink = sinks_ref[head:head + 1, :]
            sc = jnp.where(valid, swa_scores[head] * (SWA_HD ** -0.5), NEG_INF)
            m = jnp.maximum(jnp.max(sc, axis=-1, keepdims=True), sink)
            p = jnp.exp(sc - m)
            swa_den.append(jnp.sum(p, axis=-1, keepdims=True) + jnp.exp(sink - m))
            swa_p.append(p.astype(BF16))

        ret_o = [jnp.dot(ret_p[hh], vb[hh], preferred_element_type=F32) for hh in ret_heads]
        swa_o = [jnp.dot(swa_p[head], vcat_ref[kv_idx(head)], preferred_element_type=F32)
                 for head in swa_heads]

        for hh in ret_heads:
            o = ret_o[hh] + ret_cross[hh] * qdec_ref[hh]
            o = o * lax.rsqrt(jnp.mean(o * o, axis=-1, keepdims=True) + RMS_EPS)
            gate = proj_ref[rows, col(OFF_RG, hh)]
            mix_ref[rows, col(0, hh)] = (o * _silu(gate)).astype(BF16)
        for i in range(SWA_Q_HEADS // 2):
            pair = swa_o[2 * i] / swa_den[2 * i] + swa_o[2 * i + 1] / swa_den[2 * i + 1]
            mix_ref[rows, col(512, i)] = pair.astype(BF16)
        for idx in range(4):
            kcat_ref[idx, 0:CHUNK, :] = kcat_ref[idx, CHUNK:2 * CHUNK, :]
            vcat_ref[idx, 0:CHUNK, :] = vcat_ref[idx, CHUNK:2 * CHUNK, :]
        return carry

    lax.fori_loop(0, n_chunks, chunk_body, 0)

    h = x + jnp.dot(mix_ref[...], pltpu.bitcast(wout_ref[...], BF16),
                             preferred_element_type=F32)
    h_ref[...] = h
    h2t_ref[...] = pltpu.bitcast(_rms_norm(h, g2_ref[...]).T.astype(BF16), jnp.uint32)
    st_ref[...] = state_ref[...]


def _prompt_mixer(x, g1, win, sinks, wout, g2, inv, *, tq):
    batch, seq, d = x.shape
    ns = seq // tq
    full = lambda shape: pl.BlockSpec(shape, lambda b, s: (0,) * len(shape))
    return pl.pallas_call(
        functools.partial(_prompt_mixer_kernel, tq=tq),
        grid=(batch, ns),
        in_specs=[
            pl.BlockSpec((None, tq, d), lambda b, s: (b, s, 0)),
            full((1, d)), full((d // 2, D_IN)), full((SWA_Q_HEADS, 1)), full((d // 2, d)),
            full((1, d)),
            full((1, LANES)),
        ],
        out_specs=[
            pl.BlockSpec((tq, d), lambda b, s: (b * ns + s, 0)),
            pl.BlockSpec((d // 2, tq), lambda b, s: (0, b * ns + s)),
            pl.BlockSpec((None, RET_HEADS, 128, 128), lambda b, s: (b, 0, 0, 0)),
            pl.BlockSpec((None, WINDOW, 128), lambda b, s: (b, 0, 0)),
            pl.BlockSpec((None, WINDOW, 128), lambda b, s: (b, 0, 0)),
        ],
        out_shape=[
            jax.ShapeDtypeStruct((batch * seq, d), F32),
            jax.ShapeDtypeStruct((d // 2, batch * seq), jnp.uint32),
            jax.ShapeDtypeStruct((batch, RET_HEADS, 128, 128), F32),
            jax.ShapeDtypeStruct((batch, WINDOW, 128), F32),
            jax.ShapeDtypeStruct((batch, WINDOW, 128), F32),
        ],
        scratch_shapes=[
            pltpu.VMEM((tq, D_IN), F32),
            pltpu.VMEM((RET_HEADS, 128, 128), F32),
            pltpu.VMEM((4, 2 * CHUNK, LANES), BF16),
            pltpu.VMEM((4, 2 * CHUNK, LANES), BF16),
            pltpu.VMEM((tq, d), BF16),
            pltpu.VMEM((RET_HEADS, CHUNK, CHUNK), F32),
            pltpu.VMEM((RET_HEADS, CHUNK, CHUNK), F32),
            pltpu.VMEM((RET_HEADS, CHUNK, CHUNK), F32),
        ],
        compiler_params=pltpu.CompilerParams(
            dimension_semantics=("arbitrary", "arbitrary"),
            vmem_limit_bytes=VMEM_LIMIT_BYTES),
        name="prompt_mixer",
    )(x, g1, win, sinks, wout, g2, inv)


def _sample_proj_kernel(x_ref, g1_ref, win_ref, inv_ref, proj_ref):
    xn = _rms_norm(x_ref[...], g1_ref[...]).astype(BF16)
    proj = jnp.dot(xn, pltpu.bitcast(win_ref[...], BF16), preferred_element_type=F32)
    proj_ref[...] = proj
    pos = jnp.full((1, 1), float(PAST_LEN), F32)
    cos, sin_signed, even = _rotation_tables(pos * jnp.ones((1, LANES), F32), inv_ref[...])
    even_full = jnp.broadcast_to(even, (x_ref.shape[0], LANES))
    for hh in range(RET_HEADS):
        ql = slice(OFF_RQ + hh * 128, OFF_RQ + (hh + 1) * 128)
        kl = slice(OFF_RK + hh * 128, OFF_RK + (hh + 1) * 128)
        proj_ref[:, ql] = _rotate_pairs(proj[:, ql], cos, sin_signed, even_full)
        proj_ref[:, kl] = _rotate_pairs(proj[:, kl], cos, sin_signed, even_full) * (RET_DK ** -0.5)


def _sample_proj(x, g1, win, inv):
    n = x.shape[0]
    return pl.pallas_call(
        _sample_proj_kernel,
        out_shape=jax.ShapeDtypeStruct((n, D_IN), F32),
        compiler_params=pltpu.CompilerParams(vmem_limit_bytes=VMEM_LIMIT_BYTES),
        name="sample_proj",
    )(x, g1, win, inv)


def _sample_state_kernel(proj_ref, st_ref, ck_ref, cv_ref, sinks_ref,
                         stn_ref, kn_ref, vn_ref, o2_ref, sres_ref, *, group):
    row8 = lax.broadcasted_iota(jnp.int32, (8, LANES), 0)
    lane8 = lax.broadcasted_iota(jnp.int32, (8, LANES), 1)
    lo1 = lax.broadcasted_iota(jnp.int32, (1, LANES), 1) < 64
    sink_col = sinks_ref[...]

    def sample_body(b, carry):
        row = pl.ds(b, 1)
        for hh in range(RET_HEADS):
            gamma = math.exp(LOG_DECAY[hh])
            q = proj_ref[row, OFF_RQ + hh * 128:OFF_RQ + (hh + 1) * 128]
            k = proj_ref[row, OFF_RK + hh * 128:OFF_RK + (hh + 1) * 128]
            v = proj_ref[row, OFF_RV + hh * 128:OFF_RV + (hh + 1) * 128]
            state = st_ref[b, hh]
            q8 = jnp.broadcast_to(q, (8, LANES)).astype(BF16)
            o2 = jnp.dot(q8, state.astype(BF16), preferred_element_type=F32)
            o2_ref[row, hh * 128:(hh + 1) * 128] = o2[0:1, :]
            k8 = jnp.where(row8 == 0, jnp.broadcast_to(k, (8, LANES)), 0.0).astype(BF16)
            v8 = jnp.broadcast_to(v, (8, LANES)).astype(BF16)
            outer = lax.dot_general(k8, v8, TN_DIMS, preferred_element_type=F32)
            stn_ref[b, hh] = state * gamma + outer
        kn_ref[b, 0:WINDOW - 1, :] = ck_ref[b, 1:WINDOW, :]
        vn_ref[b, 0:WINDOW - 1, :] = cv_ref[b, 1:WINDOW, :]
        kn_ref[b, WINDOW - 1:WINDOW, :] = proj_ref[row, OFF_SK:OFF_SK + 128]
        vn_ref[b, WINDOW - 1:WINDOW, :] = proj_ref[row, OFF_SV:OFF_SV + 128]
        kwin = kn_ref[b].astype(BF16)
        vwin = vn_ref[b].astype(BF16)
        q_pad = jnp.zeros((8, LANES), F32)
        for qb_idx in range(SWA_Q_HEADS // 2):
            g = qb_idx // 2
            blk = proj_ref[row, OFF_SQ + qb_idx * 128:OFF_SQ + (qb_idx + 1) * 128]
            rolled = pltpu.roll(blk, 64, 1)
            zero = jnp.zeros_like(blk)
            if g == 0:
                first, second = jnp.where(lo1, blk, zero), jnp.where(lo1, rolled, zero)
            else:
                first, second = jnp.where(lo1, zero, rolled), jnp.where(lo1, zero, blk)
            q_pad = jnp.where(row8 == 2 * qb_idx, jnp.broadcast_to(first, (8, LANES)), q_pad)
            q_pad = jnp.where(row8 == 2 * qb_idx + 1, jnp.broadcast_to(second, (8, LANES)), q_pad)
        sc = lax.dot_general(q_pad.astype(BF16), kwin, NT_DIMS,
                             preferred_element_type=F32) * (SWA_HD ** -0.5)
        m = jnp.maximum(jnp.max(sc, axis=-1, keepdims=True), sink_col)
        p = jnp.exp(sc - m)
        den = jnp.sum(p, axis=-1, keepdims=True) + jnp.exp(sink_col - m)
        sres_ref[b] = jnp.dot(p.astype(BF16), vwin, preferred_element_type=F32) / den
        return carry

    del lane8
    for b in range(group):
        sample_body(b, 0)


def _sample_state(proj, state, ck, cv, sinks, *, group):
    n = proj.shape[0]
    return pl.pallas_call(
        functools.partial(_sample_state_kernel, group=group),
        grid=(n // group,),
        in_specs=[
            pl.BlockSpec((group, D_IN), lambda i: (i, 0)),
            pl.BlockSpec((group, RET_HEADS, 128, 128), lambda i: (i, 0, 0, 0)),
            pl.BlockSpec((group, WINDOW, 128), lambda i: (i, 0, 0)),
            pl.BlockSpec((group, WINDOW, 128), lambda i: (i, 0, 0)),
            pl.BlockSpec((SWA_Q_HEADS, 1), lambda i: (0, 0)),
        ],
        out_specs=[
            pl.BlockSpec((group, RET_HEADS, 128, 128), lambda i: (i, 0, 0, 0)),
            pl.BlockSpec((group, WINDOW, 128), lambda i: (i, 0, 0)),
            pl.BlockSpec((group, WINDOW, 128), lambda i: (i, 0, 0)),
            pl.BlockSpec((group, 512), lambda i: (i, 0)),
            pl.BlockSpec((group, SWA_Q_HEADS, 128), lambda i: (i, 0, 0)),
        ],
        out_shape=[
            jax.ShapeDtypeStruct((n, RET_HEADS, 128, 128), F32),
            jax.ShapeDtypeStruct((n, WINDOW, 128), F32),
            jax.ShapeDtypeStruct((n, WINDOW, 128), F32),
            jax.ShapeDtypeStruct((n, 512), F32),
            jax.ShapeDtypeStruct((n, SWA_Q_HEADS, 128), F32),
        ],
        compiler_params=pltpu.CompilerParams(
            dimension_semantics=("arbitrary",), vmem_limit_bytes=VMEM_LIMIT_BYTES),
        name="sample_state",
    )(proj, state, ck, cv, sinks)


def _sample_finish_kernel(x_ref, proj_ref, o2_ref, sres_ref, wout_ref, g2_ref,
                          h_ref, h2t_ref, mix_ref):
    n = x_ref.shape[0]
    for hh in range(RET_HEADS):
        gamma = math.exp(LOG_DECAY[hh])
        lanes = slice(hh * 128, (hh + 1) * 128)
        q = proj_ref[:, OFF_RQ + hh * 128:OFF_RQ + (hh + 1) * 128]
        k = proj_ref[:, OFF_RK + hh * 128:OFF_RK + (hh + 1) * 128]
        v = proj_ref[:, OFF_RV + hh * 128:OFF_RV + (hh + 1) * 128]
        gate = proj_ref[:, OFF_RG + hh * 128:OFF_RG + (hh + 1) * 128]
        qk = jnp.sum(q * k, axis=-1, keepdims=True)
        o = qk * v + o2_ref[:, lanes] * gamma
        o = o * lax.rsqrt(jnp.mean(o * o, axis=-1, keepdims=True) + RMS_EPS)
        mix_ref[:, lanes] = (o * _silu(gate)).astype(BF16)
    lo = lax.broadcasted_iota(jnp.int32, (n, LANES), 1) < 64
    for qb_idx in range(SWA_Q_HEADS // 2):
        r0 = sres_ref[:, (2 * qb_idx) * 128:(2 * qb_idx + 1) * 128]
        r1 = sres_ref[:, (2 * qb_idx + 1) * 128:(2 * qb_idx + 2) * 128]
        if qb_idx // 2 == 0:
            blk = jnp.where(lo, r0, pltpu.roll(r1, 64, 1))
        else:
            blk = jnp.where(lo, pltpu.roll(r0, 64, 1), r1)
        mix_ref[:, 512 + qb_idx * 128:512 + (qb_idx + 1) * 128] = blk.astype(BF16)
    h = x_ref[...] + jnp.dot(mix_ref[...], pltpu.bitcast(wout_ref[...], BF16),
                             preferred_element_type=F32)
    h_ref[...] = h
    h2t_ref[...] = pltpu.bitcast(_rms_norm(h, g2_ref[...]).T.astype(BF16), jnp.uint32)


def _sample_finish(x, proj, o2, sres, wout, g2):
    n, d = x.shape
    return pl.pallas_call(
        _sample_finish_kernel,
        out_shape=[jax.ShapeDtypeStruct((n, d), F32),
                   jax.ShapeDtypeStruct((d // 2, n), jnp.uint32)],
        scratch_shapes=[pltpu.VMEM((n, d), BF16)],
        compiler_params=pltpu.CompilerParams(vmem_limit_bytes=VMEM_LIMIT_BYTES),
        name="sample_finish",
    )(x, proj, o2, sres, wout, g2)


def _sorting_network(n):
    def merge(lo, hi, r):
        step = r * 2
        if step < hi - lo:
            yield from merge(lo, hi, step)
            yield from merge(lo + r, hi, step)
            yield from [(i, i + r) for i in range(lo + r, hi - r, step)]
        else:
            yield (lo, lo + r)

    def sort(lo, hi):
        if hi - lo >= 1:
            mid = lo + (hi - lo) // 2
            yield from sort(lo, mid)
            yield from sort(mid + 1, hi)
            yield from merge(lo, hi, 1)

    size = 1
    while size < n:
        size *= 2
    return tuple((i, j) for i, j in sort(0, size - 1) if j < n)


def _top_values(lists, count):
    lists = list(lists)
    for i, j in _sorting_network(len(lists)):
        lists[i], lists[j] = jnp.maximum(lists[i], lists[j]), jnp.minimum(lists[i], lists[j])
    rows = []
    for p in range(count):
        m = jnp.max(lists[0], axis=0, keepdims=True)
        rows.append(m)
        if p + 1 < count:
            hit = lists[0] == m
            depth = min(len(lists), count - p)
            for k in range(depth - 1):
                lists[k] = jnp.where(hit, lists[k + 1], lists[k])
            if depth == len(lists):
                lists[depth - 1] = jnp.where(hit, NEG_INF, lists[depth - 1])
    return rows


def _sublane_lists(s):
    return [s[k * 8:(k + 1) * 8, :] for k in range(s.shape[0] // 8)]


def _stack8(rows, row8):
    out = jnp.zeros(row8.shape, F32)
    for r, v in enumerate(rows):
        out = jnp.where(row8 == r, jnp.broadcast_to(v, row8.shape), out)
    return out


def _routing_tables(s0, s1):
    a = _top_values(_sublane_lists(s0), PEER_TOPK)
    b = _top_values(_sublane_lists(s1), PEER_TOPK)
    row8 = lax.broadcasted_iota(jnp.int32, (8, s0.shape[1]), 0)
    a_lo, a_hi = _stack8(a[:8], row8), _stack8(a[8:], row8)
    b_lo, b_hi = _stack8(b[:8], row8), _stack8(b[8:], row8)
    neg = jnp.full(row8.shape, NEG_INF, F32)
    cands = [
        a[0] + b_lo, a[0] + b_hi, a[1] + b_lo,
        jnp.where(row8 < 5, a[2] + b_lo, neg),
        jnp.where(row8 < 4, a[3] + b_lo, neg),
        b[0] + a_hi,
        jnp.where(row8 >= 4, b[0] + a_lo, neg),
        jnp.where(row8 >= 4, b[1] + a_lo, neg),
        jnp.where(row8 == 4, b[2] + a_lo, neg),
    ]
    tau = _top_values(cands, PEER_TOPK)[-1]
    top = a[0] + b[0]
    passed = [c >= tau for c in cands]
    col_sum = lambda x: jnp.sum(x, axis=0, keepdims=True)
    z = functools.reduce(lambda x, y: x + y, [
        col_sum(jnp.where(ok, jnp.exp(c - top), 0.0)) for ok, c in zip(passed, cands)])
    cnt = [jnp.where(ok, 1.0, 0.0) for ok in passed]
    by_row = cnt[6] + cnt[7] + cnt[8]
    n = [col_sum(cnt[0]) + col_sum(cnt[1]), col_sum(cnt[2]), col_sum(cnt[3]), col_sum(cnt[4])]
    n += [by_row[p:p + 1, :] for p in range(4, 8)]
    n += [cnt[5][p:p + 1, :] for p in range(8)]
    n0 = jnp.zeros(s0.shape, F32)
    rank1 = jnp.zeros(s1.shape, F32)
    for p in range(PEER_TOPK):
        n0 = jnp.where(s0 == a[p], n[p], n0)
        rank1 = jnp.where(s1 < b[p], p + 1.0, rank1)
    e1 = jnp.exp(s1 - b[0])
    e0n = jnp.exp(s0 - a[0]) * (1.0 / z)
    return rank1, e1, n0, e0n


def _routing_kernel(h2t_ref, wqt_ref, keys_ref, r1_ref, e1_ref, n0_ref, e0_ref,
                    q_ref, s_ref, *, tt):
    q_ref[...] = jnp.dot(pltpu.bitcast(wqt_ref[...], BF16), pltpu.bitcast(h2t_ref[...], BF16),
                         preferred_element_type=F32)
    n_lg = tt // LANES

    def head_body(h, carry):
        for c in range(2):
            r0 = pl.multiple_of((2 * h + c) * 128, 128)
            q_hc = q_ref[pl.ds(r0, 128), :].astype(BF16)
            key_words = keys_ref[pl.ds(pl.multiple_of((2 * h + c) * 64, 64), 64), :]
            s_ref[c] = jnp.dot(pltpu.bitcast(key_words, BF16), q_hc, preferred_element_type=F32)
        out_rows = pl.ds(pl.multiple_of(h * 128, 128), 128)
        packed_rows = pl.ds(pl.multiple_of(h * 64, 64), 64)
        rank1, e1, n0, e0n = _routing_tables(s_ref[0], s_ref[1])
        for lg in range(n_lg):
            lanes = slice(lg * LANES, (lg + 1) * LANES)
            r1_ref[lg, packed_rows, :] = pltpu.bitcast(rank1[:, lanes].astype(BF16), jnp.uint32)
            e1_ref[lg, packed_rows, :] = pltpu.bitcast(e1[:, lanes].astype(BF16), jnp.uint32)
            n0_ref[lg, out_rows, :] = n0[:, lanes]
            e0_ref[lg, out_rows, :] = e0n[:, lanes]
        return carry

    lax.fori_loop(0, PEER_HEADS, head_body, 0)


def _routing(h2t, wqt, keys, *, tt):
    d, t = 2 * h2t.shape[0], h2t.shape[1]
    rows = PEER_HEADS * PEER_N_KEYS
    tab = lambda r, dt: jax.ShapeDtypeStruct((t // LANES, r, LANES), dt)
    tab_spec = lambda r: pl.BlockSpec((tt // LANES, r, LANES), lambda i: (i, 0, 0))
    return pl.pallas_call(
        functools.partial(_routing_kernel, tt=tt),
        grid=(t // tt,),
        in_specs=[
            pl.BlockSpec((d // 2, tt), lambda i: (0, i)),
            pl.BlockSpec(wqt.shape, lambda i: (0, 0)),
            pl.BlockSpec(keys.shape, lambda i: (0, 0)),
        ],
        out_specs=[tab_spec(rows // 2), tab_spec(rows // 2), tab_spec(rows), tab_spec(rows)],
        out_shape=[tab(rows // 2, jnp.uint32), tab(rows // 2, jnp.uint32),
                   tab(rows, F32), tab(rows, F32)],
        scratch_shapes=[pltpu.VMEM((2 * wqt.shape[0], tt), F32), pltpu.VMEM((2, 128, tt), F32)],
        compiler_params=pltpu.CompilerParams(
            dimension_semantics=("arbitrary",), vmem_limit_bytes=VMEM_LIMIT_BYTES),
        name="peer_routing",
    )(h2t, wqt, keys)


def _experts_kernel(xt_ref, u_ref, vt_ref, r1_ref, e1_ref, n0_ref, e0_ref, h_ref, fg_ref,
                    y_ref, acc_ref, hd_ref, at_ref, *, tt, et, n_e, n_tiles):
    g = pl.program_id(0)
    e_c = jnp.clip(g - 2, 0, n_tiles - 1) % n_e

    @pl.when(g == 0)
    def _prime():
        hd_ref[...] = jnp.zeros_like(hd_ref)
        at_ref[...] = jnp.zeros_like(at_ref)

    @pl.when(e_c == 0)
    def _new_token_tile():
        acc_ref[...] = jnp.zeros_like(acc_ref)

    stages = functools.partial(_experts_stages, xt_ref, u_ref, vt_ref, r1_ref, e1_ref, n0_ref,
                               e0_ref, acc_ref, hd_ref, at_ref, tt=tt, et=et, n_e=n_e,
                               n_tiles=n_tiles)
    pl.when(g % 2 == 0)(functools.partial(stages, slot_a=0))
    pl.when(g % 2 == 1)(functools.partial(stages, slot_a=1))

    @pl.when((g >= 2) & (e_c == n_e - 1))
    def _finish():
        y_ref[...] = _rms_norm(h_ref[...] + acc_ref[...].T, fg_ref[...])


def _experts_stages(xt_ref, u_ref, vt_ref, r1_ref, e1_ref, n0_ref, e0_ref, acc_ref, hd_ref,
                    at_ref, *, tt, et, n_e, n_tiles, slot_a):
    slot_b = 1 - slot_a
    n_ib = et // PEER_N_KEYS
    sub = 16
    e_b = jnp.clip(pl.program_id(0) - 1, 0, n_tiles - 1) % n_e

    def packed(rows):
        return slice(rows.start // 2, rows.stop // 2)

    def stage_a(rows):
        u = pltpu.bitcast(u_ref[packed(rows), :], BF16)
        xt = pltpu.bitcast(xt_ref[...], BF16)
        hd_ref[slot_a, rows, :] = jnp.dot(u, xt, preferred_element_type=F32)

    def stage_b(ii, lg):
        lanes = slice(lg * LANES, (lg + 1) * LANES)
        rows = slice(ii * 128, (ii + 1) * 128)
        w = jnp.zeros((PEER_N_KEYS, LANES), BF16)
        for h in range(PEER_HEADS):
            grp = pl.ds(pl.multiple_of(h * 128 + e_b * n_ib + (ii // 8) * 8, 8), 8)
            n0b = jnp.broadcast_to(n0_ref[lg, grp, :][ii % 8:ii % 8 + 1, :], (sub, LANES))
            e0b = jnp.broadcast_to(e0_ref[lg, grp, :][ii % 8:ii % 8 + 1, :], (sub, LANES))
            n0b = jnp.concatenate([n0b.astype(BF16)] * (PEER_N_KEYS // sub), axis=0)
            e0b = jnp.concatenate([e0b.astype(BF16)] * (PEER_N_KEYS // sub), axis=0)
            r1 = pltpu.bitcast(r1_ref[lg, h * 64:(h + 1) * 64, :], BF16)
            e1 = pltpu.bitcast(e1_ref[lg, h * 64:(h + 1) * 64, :], BF16)
            w = w + jnp.where(r1 < n0b, e1 * e0b, jnp.zeros_like(e1))
        x = hd_ref[slot_b, rows, lanes]
        gelu = 0.5 * x * (1.0 + lax.erf(x * math.sqrt(0.5)))
        at_ref[slot_b, rows, lanes] = gelu.astype(BF16) * w

    def stage_c(rows):
        vt = pltpu.bitcast(vt_ref[packed(rows), :], BF16)
        acc_ref[rows, :] += jnp.dot(vt, at_ref[slot_a], preferred_element_type=F32)

    d = acc_ref.shape[0]
    n_lg = tt // LANES
    b_pieces = [(ii, lg) for ii in range(n_ib) for lg in range(n_lg)]
    per_chunk = len(b_pieces) // MXU_CHUNKS
    quarter = max(per_chunk // 4, 1)
    for k in range(MXU_CHUNKS):
        mine = b_pieces[k * per_chunk:(k + 1) * per_chunk]
        for piece in mine[:quarter]:
            stage_b(*piece)
        stage_a(slice(k * et // MXU_CHUNKS, (k + 1) * et // MXU_CHUNKS))
        for piece in mine[quarter:3 * quarter]:
            stage_b(*piece)
        stage_c(slice(k * d // MXU_CHUNKS, (k + 1) * d // MXU_CHUNKS))
        for piece in mine[3 * quarter:]:
            stage_b(*piece)


def _experts(xt, u, vt, r1, e1, n0, e0, h, fg, *, tt, et):
    d, t = 2 * xt.shape[0], xt.shape[1]
    n_exp = 2 * u.shape[0]
    rows = n0.shape[1]
    assert et % (8 * PEER_N_KEYS) == 0 and t % tt == 0 and n_exp % et == 0
    n_e = n_exp // et
    n_tiles = (t // tt) * n_e
    tile_a = lambda g: jnp.minimum(g, n_tiles - 1)
    tile_b = lambda g: jnp.clip(g - 1, 0, n_tiles - 1)
    tile_c = lambda g: jnp.clip(g - 2, 0, n_tiles - 1)
    tab_spec = lambda r: pl.BlockSpec((tt // LANES, r, LANES), lambda g: (tile_b(g) // n_e, 0, 0))
    return pl.pallas_call(
        functools.partial(_experts_kernel, tt=tt, et=et, n_e=n_e, n_tiles=n_tiles),
        grid=(n_tiles + 2,),
        in_specs=[
            pl.BlockSpec((d // 2, tt), lambda g: (0, tile_a(g) // n_e)),
            pl.BlockSpec((et // 2, d), lambda g: (tile_a(g) % n_e, 0)),
            pl.BlockSpec((d // 2, et), lambda g: (0, tile_c(g) % n_e)),
            tab_spec(rows // 2), tab_spec(rows // 2), tab_spec(rows), tab_spec(rows),
            pl.BlockSpec((tt, d), lambda g: (tile_c(g) // n_e, 0)),
            pl.BlockSpec((1, d), lambda g: (0, 0)),
        ],
        out_specs=pl.BlockSpec((tt, d), lambda g: (tile_c(g) // n_e, 0)),
        out_shape=jax.ShapeDtypeStruct((t, d), F32),
        scratch_shapes=[
            pltpu.VMEM((d, tt), F32),
            pltpu.VMEM((2, et, tt), F32),
            pltpu.VMEM((2, et, tt), BF16),
        ],
        compiler_params=pltpu.CompilerParams(
            dimension_semantics=("arbitrary",),
            vmem_limit_bytes=VMEM_LIMIT_BYTES),
        name="peer_experts",
    )(xt, u, vt, r1, e1, n0, e0, h, fg)


def _pack_kernel(x_ref, o_ref, *, transpose):
    x = x_ref[...]
    if transpose:
        x = x.T
    o_ref[...] = pltpu.bitcast(x.astype(BF16), jnp.uint32)


def _pack_bf16(x, *, transpose=False, block_rows=1024):
    m, n = x.shape
    bm = min(block_rows, m)
    assert m % bm == 0
    if transpose:
        out_shape, out_spec = (n // 2, m), pl.BlockSpec((n // 2, bm), lambda i: (0, i))
    else:
        out_shape, out_spec = (m // 2, n), pl.BlockSpec((bm // 2, n), lambda i: (i, 0))
    return pl.pallas_call(
        functools.partial(_pack_kernel, transpose=transpose),
        grid=(m // bm,),
        in_specs=[pl.BlockSpec((bm, n), lambda i: (i, 0))],
        out_specs=out_spec,
        out_shape=jax.ShapeDtypeStruct(out_shape, jnp.uint32),
        compiler_params=pltpu.CompilerParams(
            dimension_semantics=("arbitrary",), vmem_limit_bytes=VMEM_LIMIT_BYTES),
        name="pack_bf16",
    )(x)


def _peer_and_final(h, h2t, wqt, keys, u, vt, fg, *, tt_route, tt, et):
    r1, e1, n0, e0 = _routing(h2t, wqt, keys, tt=tt_route)
    return _experts(h2t, u, vt, r1, e1, n0, e0, h, fg, tt=tt, et=et)


def kernel(x_prompt, x_sample, state_ret, cache_swa_k, cache_swa_v, norm1_g, w_in, swa_sinks, w_out, norm2_g, peer_w_q, peer_sub_keys, peer_u, peer_v, final_g):
    batch, seq, d = x_prompt.shape
    n_s = x_sample.shape[0]
    g1 = norm1_g.reshape(1, d)
    g2 = norm2_g.reshape(1, d)
    fg = final_g.reshape(1, d)
    sinks = swa_sinks.astype(F32).reshape(SWA_Q_HEADS, 1)
    win = _pack_bf16(w_in)
    wout = _pack_bf16(w_out)
    wqt = _pack_bf16(peer_w_q, transpose=True)
    keys = _pack_bf16(peer_sub_keys.reshape(PEER_HEADS * 2 * PEER_N_KEYS, -1))
    u = _pack_bf16(peer_u)
    vt = _pack_bf16(peer_v, transpose=True)
    half = RET_DK // 2
    inv = 1.0 / (ROPE_BASE ** jnp.linspace(0.0, 1.0, half, dtype=F32))
    inv = jnp.repeat(inv, 2).reshape(1, RET_DK)

    h_p, h2t_p, st_p, k_p, v_p = _prompt_mixer(x_prompt, g1, win, sinks, wout, g2, inv, tq=512)
    y_p = _peer_and_final(h_p, h2t_p, wqt, keys, u, vt, fg, tt_route=256, tt=512, et=1024)

    xs = x_sample.reshape(n_s, d)
    proj = _sample_proj(xs, g1, win, inv)
    st_s, k_s, v_s, o2, sres = _sample_state(
        proj, state_ret, cache_swa_k.reshape(n_s, WINDOW, 128),
        cache_swa_v.reshape(n_s, WINDOW, 128), sinks, group=8)
    h_s, h2t_s = _sample_finish(xs, proj, o2, sres.reshape(n_s, SWA_Q_HEADS * 128), wout, g2)
    y_s = _peer_and_final(h_s, h2t_s, wqt, keys, u, vt, fg, tt_route=128, tt=128, et=1024)

    kv_shape = (WINDOW, 2, SWA_HD)
    return (y_p.reshape(batch, seq, d), y_s.reshape(n_s, 1, d), st_p,
            k_p.reshape(batch, *kv_shape), v_p.reshape(batch, *kv_shape),
            st_s, k_s.reshape(n_s, *kv_shape), v_s.reshape(n_s, *kv_shape))
```

```python
import functools
import math

import jax
import jax.numpy as jnp
import numpy as np
from jax import lax
from jax.experimental import pallas as pl
from jax.experimental.pallas import tpu as pltpu

F32 = jnp.float32
BF16 = jnp.bfloat16

D_MODEL = 1024
SEQ = 8192
PAST_LEN = 8192
RET_HEADS = 4
RET_DK = 128
CHUNK = 128
ROPE_BASE = 10000.0
SWA_Q_HEADS = 8
SWA_HD = 64
WINDOW = 128
D_IN = 2816
OFF_RQ, OFF_RK, OFF_RV, OFF_RG, OFF_SQ, OFF_SK, OFF_SV = 0, 512, 1024, 1536, 2048, 2560, 2688
PEER_HEADS = 8
PEER_N_KEYS = 128
PEER_N_EXPERTS = PEER_N_KEYS * PEER_N_KEYS
PEER_TOPK = 16
RMS_EPS = 1e-6
LANES = 128

LOG_DECAY = tuple(math.log(1.0 - 2.0 ** (-5.0 - h)) for h in range(RET_HEADS))

VMEM_LIMIT_BYTES = 60 * 1024 * 1024
MXU_ROWS = 256

NEG_INF = float("-inf")
NT_DIMS = (((1,), (1,)), ((), ()))
TN_DIMS = (((0,), (0,)), ((), ()))


def _rms_norm(x, g):
    return x * lax.rsqrt(jnp.mean(x * x, axis=-1, keepdims=True) + RMS_EPS) * g


def _rotation_tables(pos, inv):
    ang = pos * inv
    cos = jnp.cos(ang)
    sin = jnp.sin(ang)
    lane = lax.broadcasted_iota(jnp.int32, ang.shape, 1)
    even = (lane & 1) == 0
    return cos, jnp.where(even, -sin, sin), even


def _rotate_pairs(x, cos, sin_signed, even):
    partner = jnp.where(even, pltpu.roll(x, LANES - 1, 1), pltpu.roll(x, 1, 1))
    return x * cos + partner * sin_signed


def _silu(g):
    return g * jax.nn.sigmoid(g)


def _head_half_variants(a, lo):
    rolled = pltpu.roll(a, 64, 1)
    zero = jnp.zeros_like(a)
    return (jnp.where(lo, a, zero), jnp.where(lo, zero, rolled),
            jnp.where(lo, rolled, zero), jnp.where(lo, zero, a))


def _prompt_mixer_kernel(x_ref, g1_ref, win_ref, sinks_ref, wout_ref, g2_ref, inv_ref,
                         h_ref, h2t_ref, st_ref, kp_ref, vp_ref,
                         proj_ref, state_ref, kcat_ref, vcat_ref, mix_ref,
                         dmask_ref, qdec_ref, kdec_ref, *, tq):
    s = pl.program_id(1)
    n_chunks = tq // CHUNK

    @pl.when(s == 0)
    def _init():
        state_ref[...] = jnp.zeros_like(state_ref)
        kcat_ref[...] = jnp.zeros_like(kcat_ref)
        vcat_ref[...] = jnp.zeros_like(vcat_ref)
        t_row = lax.broadcasted_iota(jnp.int32, (CHUNK, CHUNK), 0).astype(F32)
        t_col = lax.broadcasted_iota(jnp.int32, (CHUNK, CHUNK), 1).astype(F32)
        diff = t_row - t_col
        causal = diff >= 0
        for hh in range(RET_HEADS):
            lg = LOG_DECAY[hh]
            dmask_ref[hh] = jnp.where(causal, jnp.exp(jnp.where(causal, diff, 0.0) * lg), 0.0)
            qdec_ref[hh] = jnp.exp((t_row + 1.0) * lg)
            kdec_ref[hh] = jnp.exp((CHUNK - 1.0 - t_row) * lg)

    x = x_ref[...]
    xn = _rms_norm(x, g1_ref[...]).astype(BF16)
    proj_ref[...] = jnp.dot(xn, pltpu.bitcast(win_ref[...], BF16), preferred_element_type=F32)

    def chunk_body(c, carry):
        r0 = pl.multiple_of(c * CHUNK, CHUNK)
        rows = pl.ds(r0, CHUNK)
        chunk_idx = s * n_chunks + c
        t_local = lax.broadcasted_iota(jnp.int32, (CHUNK, 1), 0)
        pos = (chunk_idx * CHUNK + t_local).astype(F32)
        cos, sin_signed, even = _rotation_tables(pos, inv_ref[...])

        ret_heads = range(RET_HEADS)
        swa_heads = range(SWA_Q_HEADS)

        col = lambda off, hh: slice(off + hh * 128, off + (hh + 1) * 128)
        k_ret = [_rotate_pairs(proj_ref[rows, col(OFF_RK, hh)], cos, sin_signed, even)
                 * (RET_DK ** -0.5) for hh in ret_heads]
        qb = [_rotate_pairs(proj_ref[rows, col(OFF_RQ, hh)], cos, sin_signed, even).astype(BF16)
              for hh in ret_heads]
        kb = [k.astype(BF16) for k in k_ret]
        k_dec = [(k_ret[hh] * kdec_ref[hh]).astype(BF16) for hh in ret_heads]
        vb = [proj_ref[rows, col(OFF_RV, hh)].astype(BF16) for hh in ret_heads]
        kc = proj_ref[rows, OFF_SK:OFF_SK + 128]
        vc = proj_ref[rows, OFF_SV:OFF_SV + 128]
        kp_ref[...] = kc
        vp_ref[...] = vc
        lane = lax.broadcasted_iota(jnp.int32, (CHUNK, LANES), 1)
        lo = lane < 64
        for idx, (kvar, vvar) in enumerate(zip(_head_half_variants(kc, lo),
                                                _head_half_variants(vc, lo))):
            kcat_ref[idx, CHUNK:2 * CHUNK, :] = kvar.astype(BF16)
            vcat_ref[idx, CHUNK:2 * CHUNK, :] = vvar.astype(BF16)
        q_swa = [proj_ref[rows, col(OFF_SQ, i)].astype(BF16) for i in range(SWA_Q_HEADS // 2)]
        kv_idx = lambda head: (head // 4) * 2 + head % 2

        ret_scores = [lax.dot_general(qb[hh], kb[hh], NT_DIMS, preferred_element_type=F32)
                      for hh in ret_heads]
        ret_cross = [jnp.dot(qb[hh], state_ref[hh].astype(BF16), preferred_element_type=F32)
                     for hh in ret_heads]
        ret_kv = [lax.dot_general(k_dec[hh], vb[hh], TN_DIMS, preferred_element_type=F32)
                  for hh in ret_heads]
        swa_scores = [lax.dot_general(q_swa[head // 2], kcat_ref[kv_idx(head)], NT_DIMS,
                                      preferred_element_type=F32) for head in swa_heads]

        for hh in ret_heads:
            state_ref[hh] = state_ref[hh] * math.exp(CHUNK * LOG_DECAY[hh]) + ret_kv[hh]
        ret_p = [(ret_scores[hh] * dmask_ref[hh]).astype(BF16) for hh in ret_heads]
        q_row = lax.broadcasted_iota(jnp.int32, (CHUNK, 2 * CHUNK), 0)
        k_col = lax.broadcasted_iota(jnp.int32, (CHUNK, 2 * CHUNK), 1)
        valid = (k_col >= q_row + 1) & (k_col <= q_row + WINDOW)
        valid = valid & ((k_col >= CHUNK) | (chunk_idx > 0))
        swa_p, swa_den = [], []
        for head in swa_heads:
            sink = sinks_ref[head:head + 1, :]
            sc = jnp.where(valid, swa_scores[head] * (SWA_HD ** -0.5), NEG_INF)
            m = jnp.maximum(jnp.max(sc, axis=-1, keepdims=True), sink)
            p = jnp.exp(sc - m)
            swa_den.append(jnp.sum(p, axis=-1, keepdims=True) + jnp.exp(sink - m))
            swa_p.append(p.astype(BF16))

        ret_o = [jnp.dot(ret_p[hh], vb[hh], preferred_element_type=F32) for hh in ret_heads]
        swa_o = [jnp.dot(swa_p[head], vcat_ref[kv_idx(head)], preferred_element_type=F32)
                 for head in swa_heads]

        for hh in ret_heads:
            o = ret_o[hh] + ret_cross[hh] * qdec_ref[hh]
            o = o * lax.rsqrt(jnp.mean(o * o, axis=-1, keepdims=True) + RMS_EPS)
            gate = proj_ref[rows, col(OFF_RG, hh)]
            mix_ref[rows, col(0, hh)] = (o * _silu(gate)).astype(BF16)
        for i in range(SWA_Q_HEADS // 2):
            pair = swa_o[2 * i] / swa_den[2 * i] + swa_o[2 * i + 1] / swa_den[2 * i + 1]
            mix_ref[rows, col(512, i)] = pair.astype(BF16)
        for idx in range(4):
            kcat_ref[idx, 0:CHUNK, :] = kcat_ref[idx, CHUNK:2 * CHUNK, :]
            vcat_ref[idx, 0:CHUNK, :] = vcat_ref[idx, CHUNK:2 * CHUNK, :]
        return carry

    lax.fori_loop(0, n_chunks, chunk_body, 0)

    h = x + jnp.dot(mix_ref[...], pltpu.bitcast(wout_ref[...], BF16),
                             preferred_element_type=F32)
    h_ref[...] = h
    h2t_ref[...] = pltpu.bitcast(_rms_norm(h, g2_ref[...]).T.astype(BF16), jnp.uint32)
    st_ref[...] = state_ref[...]


def _prompt_mixer(x, g1, win, sinks, wout, g2, inv, *, tq):
    batch, seq, d = x.shape
    ns = seq // tq
    full = lambda shape: pl.BlockSpec(shape, lambda b, s: (0,) * len(shape))
    return pl.pallas_call(
        functools.partial(_prompt_mixer_kernel, tq=tq),
        grid=(batch, ns),
        in_specs=[
            pl.BlockSpec((None, tq, d), lambda b, s: (b, s, 0)),
            full((1, d)), full((d // 2, D_IN)), full((SWA_Q_HEADS, 1)), full((d // 2, d)),
            full((1, d)),
            full((1, LANES)),
        ],
        out_specs=[
            pl.BlockSpec((tq, d), lambda b, s: (b * ns + s, 0)),
            pl.BlockSpec((d // 2, tq), lambda b, s: (0, b * ns + s)),
            pl.BlockSpec((None, RET_HEADS, 128, 128), lambda b, s: (b, 0, 0, 0)),
            pl.BlockSpec((None, WINDOW, 128), lambda b, s: (b, 0, 0)),
            pl.BlockSpec((None, WINDOW, 128), lambda b, s: (b, 0, 0)),
        ],
        out_shape=[
            jax.ShapeDtypeStruct((batch * seq, d), F32),
            jax.ShapeDtypeStruct((d // 2, batch * seq), jnp.uint32),
            jax.ShapeDtypeStruct((batch, RET_HEADS, 128, 128), F32),
            jax.ShapeDtypeStruct((batch, WINDOW, 128), F32),
            jax.ShapeDtypeStruct((batch, WINDOW, 128), F32),
        ],
        scratch_shapes=[
            pltpu.VMEM((tq, D_IN), F32),
            pltpu.VMEM((RET_HEADS, 128, 128), F32),
            pltpu.VMEM((4, 2 * CHUNK, LANES), BF16),
            pltpu.VMEM((4, 2 * CHUNK, LANES), BF16),
            pltpu.VMEM((tq, d), BF16),
            pltpu.VMEM((RET_HEADS, CHUNK, CHUNK), F32),
            pltpu.VMEM((RET_HEADS, CHUNK, CHUNK), F32),
            pltpu.VMEM((RET_HEADS, CHUNK, CHUNK), F32),
        ],
        compiler_params=pltpu.CompilerParams(
            dimension_semantics=("arbitrary", "arbitrary"),
            vmem_limit_bytes=VMEM_LIMIT_BYTES),
        name="prompt_mixer",
    )(x, g1, win, sinks, wout, g2, inv)


def _sample_proj_kernel(x_ref, g1_ref, win_ref, inv_ref, proj_ref):
    xn = _rms_norm(x_ref[...], g1_ref[...]).astype(BF16)
    proj = jnp.dot(xn, pltpu.bitcast(win_ref[...], BF16), preferred_element_type=F32)
    proj_ref[...] = proj
    pos = jnp.full((1, 1), float(PAST_LEN), F32)
    cos, sin_signed, even = _rotation_tables(pos * jnp.ones((1, LANES), F32), inv_ref[...])
    even_full = jnp.broadcast_to(even, (x_ref.shape[0], LANES))
    for hh in range(RET_HEADS):
        ql = slice(OFF_RQ + hh * 128, OFF_RQ + (hh + 1) * 128)
        kl = slice(OFF_RK + hh * 128, OFF_RK + (hh + 1) * 128)
        proj_ref[:, ql] = _rotate_pairs(proj[:, ql], cos, sin_signed, even_full)
        proj_ref[:, kl] = _rotate_pairs(proj[:, kl], cos, sin_signed, even_full) * (RET_DK ** -0.5)


def _sample_proj(x, g1, win, inv):
    n = x.shape[0]
    return pl.pallas_call(
        _sample_proj_kernel,
        out_shape=jax.ShapeDtypeStruct((n, D_IN), F32),
        compiler_params=pltpu.CompilerParams(vmem_limit_bytes=VMEM_LIMIT_BYTES),
        name="sample_proj",
    )(x, g1, win, inv)


def _sample_state_kernel(proj_ref, st_ref, ck_ref, cv_ref, sinks_ref,
                         stn_ref, kn_ref, vn_ref, o2_ref, sres_ref, *, group):
    row8 = lax.broadcasted_iota(jnp.int32, (8, LANES), 0)
    lane8 = lax.broadcasted_iota(jnp.int32, (8, LANES), 1)
    lo1 = lax.broadcasted_iota(jnp.int32, (1, LANES), 1) < 64
    sink_col = sinks_ref[...]

    def sample_body(b, carry):
        row = pl.ds(b, 1)
        for hh in range(RET_HEADS):
            gamma = math.exp(LOG_DECAY[hh])
            q = proj_ref[row, OFF_RQ + hh * 128:OFF_RQ + (hh + 1) * 128]
            k = proj_ref[row, OFF_RK + hh * 128:OFF_RK + (hh + 1) * 128]
            v = proj_ref[row, OFF_RV + hh * 128:OFF_RV + (hh + 1) * 128]
            state = st_ref[b, hh]
            q8 = jnp.broadcast_to(q, (8, LANES)).astype(BF16)
            o2 = jnp.dot(q8, state.astype(BF16), preferred_element_type=F32)
            o2_ref[row, hh * 128:(hh + 1) * 128] = o2[0:1, :]
            k8 = jnp.where(row8 == 0, jnp.broadcast_to(k, (8, LANES)), 0.0).astype(BF16)
            v8 = jnp.broadcast_to(v, (8, LANES)).astype(BF16)
            outer = lax.dot_general(k8, v8, TN_DIMS, preferred_element_type=F32)
            stn_ref[b, hh] = state * gamma + outer
        kn_ref[b, 0:WINDOW - 1, :] = ck_ref[b, 1:WINDOW, :]
        vn_ref[b, 0:WINDOW - 1, :] = cv_ref[b, 1:WINDOW, :]
        kn_ref[b, WINDOW - 1:WINDOW, :] = proj_ref[row, OFF_SK:OFF_SK + 128]
        vn_ref[b, WINDOW - 1:WINDOW, :] = proj_ref[row, OFF_SV:OFF_SV + 128]
        kwin = kn_ref[b].astype(BF16)
        vwin = vn_ref[b].astype(BF16)
        q_pad = jnp.zeros((8, LANES), F32)
        for qb_idx in range(SWA_Q_HEADS // 2):
            g = qb_idx // 2
            blk = proj_ref[row, OFF_SQ + qb_idx * 128:OFF_SQ + (qb_idx + 1) * 128]
            rolled = pltpu.roll(blk, 64, 1)
            zero = jnp.zeros_like(blk)
            if g == 0:
                first, second = jnp.where(lo1, blk, zero), jnp.where(lo1, rolled, zero)
            else:
                first, second = jnp.where(lo1, zero, rolled), jnp.where(lo1, zero, blk)
            q_pad = jnp.where(row8 == 2 * qb_idx, jnp.broadcast_to(first, (8, LANES)), q_pad)
            q_pad = jnp.where(row8 == 2 * qb_idx + 1, jnp.broadcast_to(second, (8, LANES)), q_pad)
        sc = lax.dot_general(q_pad.astype(BF16), kwin, NT_DIMS,
                             preferred_element_type=F32) * (SWA_HD ** -0.5)
        m = jnp.maximum(jnp.max(sc, axis=-1, keepdims=True), sink_col)
        p = jnp.exp(sc - m)
        den = jnp.sum(p, axis=-1, keepdims=True) + jnp.exp(sink_col - m)
        sres_ref[b] = jnp.dot(p.astype(BF16), vwin, preferred_element_type=F32) / den
        return carry

    del lane8
    for b in range(group):
        sample_body(b, 0)


def _sample_state(proj, state, ck, cv, sinks, *, group):
    n = proj.shape[0]
    return pl.pallas_call(
        functools.partial(_sample_state_kernel, group=group),
        grid=(n // group,),
        in_specs=[
            pl.BlockSpec((group, D_IN), lambda i: (i, 0)),
            pl.BlockSpec((group, RET_HEADS, 128, 128), lambda i: (i, 0, 0, 0)),
            pl.BlockSpec((group, WINDOW, 128), lambda i: (i, 0, 0)),
            pl.BlockSpec((group, WINDOW, 128), lambda i: (i, 0, 0)),
            pl.BlockSpec((SWA_Q_HEADS, 1), lambda i: (0, 0)),
        ],
        out_specs=[
            pl.BlockSpec((group, RET_HEADS, 128, 128), lambda i: (i, 0, 0, 0)),
            pl.BlockSpec((group, WINDOW, 128), lambda i: (i, 0, 0)),
            pl.BlockSpec((group, WINDOW, 128), lambda i: (i, 0, 0)),
            pl.BlockSpec((group, 512), lambda i: (i, 0)),
            pl.BlockSpec((group, SWA_Q_HEADS, 128), lambda i: (i, 0, 0)),
        ],
        out_shape=[
            jax.ShapeDtypeStruct((n, RET_HEADS, 128, 128), F32),
            jax.ShapeDtypeStruct((n, WINDOW, 128), F32),
            jax.ShapeDtypeStruct((n, WINDOW, 128), F32),
            jax.ShapeDtypeStruct((n, 512), F32),
            jax.ShapeDtypeStruct((n, SWA_Q_HEADS, 128), F32),
        ],
        compiler_params=pltpu.CompilerParams(
            dimension_semantics=("arbitrary",), vmem_limit_bytes=VMEM_LIMIT_BYTES),
        name="sample_state",
    )(proj, state, ck, cv, sinks)


def _sample_finish_kernel(x_ref, proj_ref, o2_ref, sres_ref, wout_ref, g2_ref,
                          h_ref, h2t_ref, mix_ref):
    n = x_ref.shape[0]
    for hh in range(RET_HEADS):
        gamma = math.exp(LOG_DECAY[hh])
        lanes = slice(hh * 128, (hh + 1) * 128)
        q = proj_ref[:, OFF_RQ + hh * 128:OFF_RQ + (hh + 1) * 128]
        k = proj_ref[:, OFF_RK + hh * 128:OFF_RK + (hh + 1) * 128]
        v = proj_ref[:, OFF_RV + hh * 128:OFF_RV + (hh + 1) * 128]
        gate = proj_ref[:, OFF_RG + hh * 128:OFF_RG + (hh + 1) * 128]
        qk = jnp.sum(q * k, axis=-1, keepdims=True)
        o = qk * v + o2_ref[:, lanes] * gamma
        o = o * lax.rsqrt(jnp.mean(o * o, axis=-1, keepdims=True) + RMS_EPS)
        mix_ref[:, lanes] = (o * _silu(gate)).astype(BF16)
    lo = lax.broadcasted_iota(jnp.int32, (n, LANES), 1) < 64
    for qb_idx in range(SWA_Q_HEADS // 2):
        r0 = sres_ref[:, (2 * qb_idx) * 128:(2 * qb_idx + 1) * 128]
        r1 = sres_ref[:, (2 * qb_idx + 1) * 128:(2 * qb_idx + 2) * 128]
        if qb_idx // 2 == 0:
            blk = jnp.where(lo, r0, pltpu.roll(r1, 64, 1))
        else:
            blk = jnp.where(lo, pltpu.roll(r0, 64, 1), r1)
        mix_ref[:, 512 + qb_idx * 128:512 + (qb_idx + 1) * 128] = blk.astype(BF16)
    h = x_ref[...] + jnp.dot(mix_ref[...], pltpu.bitcast(wout_ref[...], BF16),
                             preferred_element_type=F32)
    h_ref[...] = h
    h2t_ref[...] = pltpu.bitcast(_rms_norm(h, g2_ref[...]).T.astype(BF16), jnp.uint32)


def _sample_finish(x, proj, o2, sres, wout, g2):
    n, d = x.shape
    return pl.pallas_call(
        _sample_finish_kernel,
        out_shape=[jax.ShapeDtypeStruct((n, d), F32),
                   jax.ShapeDtypeStruct((d // 2, n), jnp.uint32)],
        scratch_shapes=[pltpu.VMEM((n, d), BF16)],
        compiler_params=pltpu.CompilerParams(vmem_limit_bytes=VMEM_LIMIT_BYTES),
        name="sample_finish",
    )(x, proj, o2, sres, wout, g2)


def _sorting_network(n):
    def merge(lo, hi, r):
        step = r * 2
        if step < hi - lo:
            yield from merge(lo, hi, step)
            yield from merge(lo + r, hi, step)
            yield from [(i, i + r) for i in range(lo + r, hi - r, step)]
        else:
            yield (lo, lo + r)

    def sort(lo, hi):
        if hi - lo >= 1:
            mid = lo + (hi - lo) // 2
            yield from sort(lo, mid)
            yield from sort(mid + 1, hi)
            yield from merge(lo, hi, 1)

    size = 1
    while size < n:
        size *= 2
    return tuple((i, j) for i, j in sort(0, size - 1) if j < n)


def _top_values(lists, count):
    lists = list(lists)
    for i, j in _sorting_network(len(lists)):
        lists[i], lists[j] = jnp.maximum(lists[i], lists[j]), jnp.minimum(lists[i], lists[j])
    rows = []
    for p in range(count):
        m = jnp.max(lists[0], axis=0, keepdims=True)
        rows.append(m)
        if p + 1 < count:
            hit = lists[0] == m
            depth = min(len(lists), count - p)
            for k in range(depth - 1):
                lists[k] = jnp.where(hit, lists[k + 1], lists[k])
            if depth == len(lists):
                lists[depth - 1] = jnp.where(hit, NEG_INF, lists[depth - 1])
    return rows


def _sublane_lists(s):
    return [s[k * 8:(k + 1) * 8, :] for k in range(s.shape[0] // 8)]


def _stack8(rows, row8):
    out = jnp.zeros(row8.shape, F32)
    for r, v in enumerate(rows):
        out = jnp.where(row8 == r, jnp.broadcast_to(v, row8.shape), out)
    return out


def _routing_tables(s0, s1):
    a = _top_values(_sublane_lists(s0), PEER_TOPK)
    b = _top_values(_sublane_lists(s1), PEER_TOPK)
    row8 = lax.broadcasted_iota(jnp.int32, (8, s0.shape[1]), 0)
    a_lo, a_hi = _stack8(a[:8], row8), _stack8(a[8:], row8)
    b_lo, b_hi = _stack8(b[:8], row8), _stack8(b[8:], row8)
    neg = jnp.full(row8.shape, NEG_INF, F32)
    cands = [
        a[0] + b_lo, a[0] + b_hi, a[1] + b_lo,
        jnp.where(row8 < 5, a[2] + b_lo, neg),
        jnp.where(row8 < 4, a[3] + b_lo, neg),
        b[0] + a_hi,
        jnp.where(row8 >= 4, b[0] + a_lo, neg),
        jnp.where(row8 >= 4, b[1] + a_lo, neg),
        jnp.where(row8 == 4, b[2] + a_lo, neg),
    ]
    tau = _top_values(cands, PEER_TOPK)[-1]
    top = a[0] + b[0]
    passed = [c >= tau for c in cands]
    col_sum = lambda x: jnp.sum(x, axis=0, keepdims=True)
    z = functools.reduce(lambda x, y: x + y, [
        col_sum(jnp.where(ok, jnp.exp(c - top), 0.0)) for ok, c in zip(passed, cands)])
    cnt = [jnp.where(ok, 1.0, 0.0) for ok in passed]
    by_row = cnt[6] + cnt[7] + cnt[8]
    n = [col_sum(cnt[0]) + col_sum(cnt[1]), col_sum(cnt[2]), col_sum(cnt[3]), col_sum(cnt[4])]
    n += [by_row[p:p + 1, :] for p in range(4, 8)]
    n += [cnt[5][p:p + 1, :] for p in range(8)]
    n0 = jnp.zeros(s0.shape, F32)
    rank1 = jnp.zeros(s1.shape, F32)
    for p in range(PEER_TOPK):
        n0 = jnp.where(s0 == a[p], n[p], n0)
        rank1 = jnp.where(s1 < b[p], p + 1.0, rank1)
    e1 = jnp.exp(s1 - b[0])
    e0n = jnp.exp(s0 - a[0]) * (1.0 / z)
    return rank1, e1, n0, e0n


def _routing_kernel(h2t_ref, wqt_ref, keys_ref, r1_ref, e1_ref, n0_ref, e0_ref,
                    q_ref, s_ref, *, tt):
    q_ref[...] = jnp.dot(pltpu.bitcast(wqt_ref[...], BF16), pltpu.bitcast(h2t_ref[...], BF16),
                         preferred_element_type=F32)
    n_lg = tt // LANES

    def head_body(h, carry):
        for c in range(2):
            r0 = pl.multiple_of((2 * h + c) * 128, 128)
            q_hc = q_ref[pl.ds(r0, 128), :].astype(BF16)
            key_words = keys_ref[pl.ds(pl.multiple_of((2 * h + c) * 64, 64), 64), :]
            s_ref[c] = jnp.dot(pltpu.bitcast(key_words, BF16), q_hc, preferred_element_type=F32)
        out_rows = pl.ds(pl.multiple_of(h * 128, 128), 128)
        packed_rows = pl.ds(pl.multiple_of(h * 64, 64), 64)
        rank1, e1, n0, e0n = _routing_tables(s_ref[0], s_ref[1])
        for lg in range(n_lg):
            lanes = slice(lg * LANES, (lg + 1) * LANES)
            r1_ref[lg, packed_rows, :] = pltpu.bitcast(rank1[:, lanes].astype(BF16), jnp.uint32)
            e1_ref[lg, packed_rows, :] = pltpu.bitcast(e1[:, lanes].astype(BF16), jnp.uint32)
            n0_ref[lg, out_rows, :] = n0[:, lanes]
            e0_ref[lg, out_rows, :] = e0n[:, lanes]
        return carry

    lax.fori_loop(0, PEER_HEADS, head_body, 0)


def _routing(h2t, wqt, keys, *, tt):
    d, t = 2 * h2t.shape[0], h2t.shape[1]
    rows = PEER_HEADS * PEER_N_KEYS
    tab = lambda r, dt: jax.ShapeDtypeStruct((t // LANES, r, LANES), dt)
    tab_spec = lambda r: pl.BlockSpec((tt // LANES, r, LANES), lambda i: (i, 0, 0))
    return pl.pallas_call(
        functools.partial(_routing_kernel, tt=tt),
        grid=(t // tt,),
        in_specs=[
            pl.BlockSpec((d // 2, tt), lambda i: (0, i)),
            pl.BlockSpec(wqt.shape, lambda i: (0, 0)),
            pl.BlockSpec(keys.shape, lambda i: (0, 0)),
        ],
        out_specs=[tab_spec(rows // 2), tab_spec(rows // 2), tab_spec(rows), tab_spec(rows)],
        out_shape=[tab(rows // 2, jnp.uint32), tab(rows // 2, jnp.uint32),
                   tab(rows, F32), tab(rows, F32)],
        scratch_shapes=[pltpu.VMEM((2 * wqt.shape[0], tt), F32), pltpu.VMEM((2, 128, tt), F32)],
        compiler_params=pltpu.CompilerParams(
            dimension_semantics=("arbitrary",), vmem_limit_bytes=VMEM_LIMIT_BYTES),
        name="peer_routing",
    )(h2t, wqt, keys)


def _experts_kernel(xt_ref, u_ref, vt_ref, r1_ref, e1_ref, n0_ref, e0_ref, h_ref, fg_ref,
                    y_ref, acc_ref, hd_ref, at_ref, *, tt, et, n_e, n_tiles):
    g = pl.program_id(0)
    e_c = jnp.clip(g - 2, 0, n_tiles - 1) % n_e

    @pl.when(g == 0)
    def _prime():
        hd_ref[...] = jnp.zeros_like(hd_ref)
        at_ref[...] = jnp.zeros_like(at_ref)

    @pl.when(e_c == 0)
    def _new_token_tile():
        acc_ref[...] = jnp.zeros_like(acc_ref)

    stages = functools.partial(_experts_stages, xt_ref, u_ref, vt_ref, r1_ref, e1_ref, n0_ref,
                               e0_ref, acc_ref, hd_ref, at_ref, tt=tt, et=et, n_e=n_e,
                               n_tiles=n_tiles)
    pl.when(g % 2 == 0)(functools.partial(stages, slot_a=0))
    pl.when(g % 2 == 1)(functools.partial(stages, slot_a=1))

    @pl.when((g >= 2) & (e_c == n_e - 1))
    def _finish():
        y_ref[...] = _rms_norm(h_ref[...] + acc_ref[...].T, fg_ref[...])


def _experts_stages(xt_ref, u_ref, vt_ref, r1_ref, e1_ref, n0_ref, e0_ref, acc_ref, hd_ref,
                    at_ref, *, tt, et, n_e, n_tiles, slot_a):
    slot_b = 1 - slot_a
    n_ib = et // PEER_N_KEYS
    sub = 16
    e_b = jnp.clip(pl.program_id(0) - 1, 0, n_tiles - 1) % n_e

    def packed(rows):
        return slice(rows.start // 2, rows.stop // 2)

    def stage_a(rows):
        u = pltpu.bitcast(u_ref[packed(rows), :], BF16)
        xt = pltpu.bitcast(xt_ref[...], BF16)
        hd_ref[slot_a, rows, :] = jnp.dot(u, xt, preferred_element_type=F32)

    def stage_b(ii, lg):
        lanes = slice(lg * LANES, (lg + 1) * LANES)
        rows = slice(ii * 128, (ii + 1) * 128)
        w = jnp.zeros((PEER_N_KEYS, LANES), BF16)
        for h in range(PEER_HEADS):
            grp = pl.ds(pl.multiple_of(h * 128 + e_b * n_ib + (ii // 8) * 8, 8), 8)
            n0b = jnp.broadcast_to(n0_ref[lg, grp, :][ii % 8:ii % 8 + 1, :], (sub, LANES))
            e0b = jnp.broadcast_to(e0_ref[lg, grp, :][ii % 8:ii % 8 + 1, :], (sub, LANES))
            n0b = jnp.concatenate([n0b.astype(BF16)] * (PEER_N_KEYS // sub), axis=0)
            e0b = jnp.concatenate([e0b.astype(BF16)] * (PEER_N_KEYS // sub), axis=0)
            r1 = pltpu.bitcast(r1_ref[lg, h * 64:(h + 1) * 64, :], BF16)
            e1 = pltpu.bitcast(e1_ref[lg, h * 64:(h + 1) * 64, :], BF16)
            w = w + jnp.where(r1 < n0b, e1 * e0b, jnp.zeros_like(e1))
        x = hd_ref[slot_b, rows, lanes]
        gelu = 0.5 * x * (1.0 + lax.erf(x * math.sqrt(0.5)))
        at_ref[slot_b, rows, lanes] = gelu.astype(BF16) * w

    def stage_c(rows):
        vt = pltpu.bitcast(vt_ref[packed(rows), :], BF16)
        acc_ref[rows, :] += jnp.dot(vt, at_ref[slot_a], preferred_element_type=F32)

    d = acc_ref.shape[0]
    n_lg = tt // LANES
    b_pieces = [(ii, lg) for ii in range(n_ib) for lg in range(n_lg)]
    pieces = sorted([((k + 0.5) * MXU_ROWS / et, 0, k) for k in range(et // MXU_ROWS)]
                    + [((k + 0.5) * MXU_ROWS / d, 1, k) for k in range(d // MXU_ROWS)])
    done = 0
    for idx, (_, is_c, k) in enumerate(pieces):
        upto = int((idx + 0.5) * len(b_pieces) / len(pieces))
        for piece in b_pieces[done:upto]:
            stage_b(*piece)
        done = max(done, upto)
        (stage_c if is_c else stage_a)(slice(k * MXU_ROWS, (k + 1) * MXU_ROWS))
    for piece in b_pieces[done:]:
        stage_b(*piece)


def _experts(xt, u, vt, r1, e1, n0, e0, h, fg, *, tt, et):
    d, t = 2 * xt.shape[0], xt.shape[1]
    n_exp = 2 * u.shape[0]
    rows = n0.shape[1]
    assert et % (8 * PEER_N_KEYS) == 0 and t % tt == 0 and n_exp % et == 0
    n_e = n_exp // et
    n_tiles = (t // tt) * n_e
    tile_a = lambda g: jnp.minimum(g, n_tiles - 1)
    tile_b = lambda g: jnp.clip(g - 1, 0, n_tiles - 1)
    tile_c = lambda g: jnp.clip(g - 2, 0, n_tiles - 1)
    tab_spec = lambda r: pl.BlockSpec((tt // LANES, r, LANES), lambda g: (tile_b(g) // n_e, 0, 0))
    return pl.pallas_call(
        functools.partial(_experts_kernel, tt=tt, et=et, n_e=n_e, n_tiles=n_tiles),
        grid=(n_tiles + 2,),
        in_specs=[
            pl.BlockSpec((d // 2, tt), lambda g: (0, tile_a(g) // n_e)),
            pl.BlockSpec((et // 2, d), lambda g: (tile_a(g) % n_e, 0)),
            pl.BlockSpec((d // 2, et), lambda g: (0, tile_c(g) % n_e)),
            tab_spec(rows // 2), tab_spec(rows // 2), tab_spec(rows), tab_spec(rows),
            pl.BlockSpec((tt, d), lambda g: (tile_c(g) // n_e, 0)),
            pl.BlockSpec((1, d), lambda g: (0, 0)),
        ],
        out_specs=pl.BlockSpec((tt, d), lambda g: (tile_c(g) // n_e, 0)),
        out_shape=jax.ShapeDtypeStruct((t, d), F32),
        scratch_shapes=[
            pltpu.VMEM((d, tt), F32),
            pltpu.VMEM((2, et, tt), F32),
            pltpu.VMEM((2, et, tt), BF16),
        ],
        compiler_params=pltpu.CompilerParams(
            dimension_semantics=("arbitrary",),
            vmem_limit_bytes=VMEM_LIMIT_BYTES),
        name="peer_experts",
    )(xt, u, vt, r1, e1, n0, e0, h, fg)


def _pack_kernel(x_ref, o_ref, *, transpose):
    x = x_ref[...]
    if transpose:
        x = x.T
    o_ref[...] = pltpu.bitcast(x.astype(BF16), jnp.uint32)


def _pack_bf16(x, *, transpose=False, block_rows=1024):
    m, n = x.shape
    bm = min(block_rows, m)
    assert m % bm == 0
    if transpose:
        out_shape, out_spec = (n // 2, m), pl.BlockSpec((n // 2, bm), lambda i: (0, i))
    else:
        out_shape, out_spec = (m // 2, n), pl.BlockSpec((bm // 2, n), lambda i: (i, 0))
    return pl.pallas_call(
        functools.partial(_pack_kernel, transpose=transpose),
        grid=(m // bm,),
        in_specs=[pl.BlockSpec((bm, n), lambda i: (i, 0))],
        out_specs=out_spec,
        out_shape=jax.ShapeDtypeStruct(out_shape, jnp.uint32),
        compiler_params=pltpu.CompilerParams(
            dimension_semantics=("arbitrary",), vmem_limit_bytes=VMEM_LIMIT_BYTES),
        name="pack_bf16",
    )(x)


def _peer_and_final(h, h2t, wqt, keys, u, vt, fg, *, tt_route, tt, et):
    r1, e1, n0, e0 = _routing(h2t, wqt, keys, tt=tt_route)
    return _experts(h2t, u, vt, r1, e1, n0, e0, h, fg, tt=tt, et=et)


def kernel(x_prompt, x_sample, state_ret, cache_swa_k, cache_swa_v, norm1_g, w_in, swa_sinks, w_out, norm2_g, peer_w_q, peer_sub_keys, peer_u, peer_v, final_g):
    batch, seq, d = x_prompt.shape
    n_s = x_sample.shape[0]
    g1 = norm1_g.reshape(1, d)
    g2 = norm2_g.reshape(1, d)
    fg = final_g.reshape(1, d)
    sinks = swa_sinks.astype(F32).reshape(SWA_Q_HEADS, 1)
    win = _pack_bf16(w_in)
    wout = _pack_bf16(w_out)
    wqt = _pack_bf16(peer_w_q, transpose=True)
    keys = _pack_bf16(peer_sub_keys.reshape(PEER_HEADS * 2 * PEER_N_KEYS, -1))
    u = _pack_bf16(peer_u)
    vt = _pack_bf16(peer_v, transpose=True)
    half = RET_DK // 2
    inv = 1.0 / (ROPE_BASE ** jnp.linspace(0.0, 1.0, half, dtype=F32))
    inv = jnp.repeat(inv, 2).reshape(1, RET_DK)

    h_p, h2t_p, st_p, k_p, v_p = _prompt_mixer(x_prompt, g1, win, sinks, wout, g2, inv, tq=512)
    y_p = _peer_and_final(h_p, h2t_p, wqt, keys, u, vt, fg, tt_route=1024, tt=512, et=2048)

    xs = x_sample.reshape(n_s, d)
    proj = _sample_proj(xs, g1, win, inv)
    st_s, k_s, v_s, o2, sres = _sample_state(
        proj, state_ret, cache_swa_k.reshape(n_s, WINDOW, 128),
        cache_swa_v.reshape(n_s, WINDOW, 128), sinks, group=8)
    h_s, h2t_s = _sample_finish(xs, proj, o2, sres.reshape(n_s, SWA_Q_HEADS * 128), wout, g2)
    y_s = _peer_and_final(h_s, h2t_s, wqt, keys, u, vt, fg, tt_route=128, tt=128, et=1024)

    kv_shape = (WINDOW, 2, SWA_HD)
    return (y_p.reshape(batch, seq, d), y_s.reshape(n_s, 1, d), st_p,
            k_p.reshape(batch, *kv_shape), v_p.reshape(batch, *kv_shape),
            st_s, k_s.reshape(n_s, *kv_shape), v_s.reshape(n_s, *kv_shape))
```

```python
import functools
import math

import jax
import jax.numpy as jnp
import numpy as np
from jax import lax
from jax.experimental import pallas as pl
from jax.experimental.pallas import tpu as pltpu

F32 = jnp.float32
BF16 = jnp.bfloat16

D_MODEL = 1024
SEQ = 8192
PAST_LEN = 8192
RET_HEADS = 4
RET_DK = 128
CHUNK = 128
ROPE_BASE = 10000.0
SWA_Q_HEADS = 8
SWA_HD = 64
WINDOW = 128
D_IN = 2816
OFF_RQ, OFF_RK, OFF_RV, OFF_RG, OFF_SQ, OFF_SK, OFF_SV = 0, 512, 1024, 1536, 2048, 2560, 2688
PEER_HEADS = 8
PEER_N_KEYS = 128
PEER_N_EXPERTS = PEER_N_KEYS * PEER_N_KEYS
PEER_TOPK = 16
RMS_EPS = 1e-6
LANES = 128

LOG_DECAY = tuple(math.log(1.0 - 2.0 ** (-5.0 - h)) for h in range(RET_HEADS))

VMEM_LIMIT_BYTES = 60 * 1024 * 1024
MXU_ROWS = 256

PROMPT_MIXER_TOKENS = 512
PROMPT_ROUTING_TOKENS = 1024
PROMPT_EXPERT_TOKENS = 512
PROMPT_EXPERTS_PER_STEP = 2048
SAMPLE_EXPERTS_PER_STEP = 1024
SAMPLE_STATE_GROUP = 8
KEY_WORDS = PEER_N_KEYS // 2

NEG_INF = float("-inf")
NT_DIMS = (((1,), (1,)), ((), ()))
TN_DIMS = (((0,), (0,)), ((), ()))


def _rms_norm(x, g):
    return x * lax.rsqrt(jnp.mean(x * x, axis=-1, keepdims=True) + RMS_EPS) * g


def _rotation_tables(pos, inv):
    ang = pos * inv
    cos = jnp.cos(ang)
    sin = jnp.sin(ang)
    lane = lax.broadcasted_iota(jnp.int32, ang.shape, 1)
    even = (lane & 1) == 0
    return cos, jnp.where(even, -sin, sin), even


def _rotate_pairs(x, cos, sin_signed, even):
    partner = jnp.where(even, pltpu.roll(x, LANES - 1, 1), pltpu.roll(x, 1, 1))
    return x * cos + partner * sin_signed


def _silu(g):
    return g * jax.nn.sigmoid(g)


def _head_half_variants(a, lo):
    rolled = pltpu.roll(a, 64, 1)
    zero = jnp.zeros_like(a)
    return (jnp.where(lo, a, zero), jnp.where(lo, zero, rolled),
            jnp.where(lo, rolled, zero), jnp.where(lo, zero, a))


def _prompt_mixer_kernel(x_ref, g1_ref, win_ref, sinks_ref, wout_ref, g2_ref, inv_ref,
                         h_ref, h2t_ref, st_ref, kp_ref, vp_ref,
                         proj_ref, state_ref, kcat_ref, vcat_ref, mix_ref,
                         dmask_ref, qdec_ref, kdec_ref, *, tq):
    s = pl.program_id(1)
    n_chunks = tq // CHUNK

    @pl.when(s == 0)
    def _init():
        state_ref[...] = jnp.zeros_like(state_ref)
        kcat_ref[...] = jnp.zeros_like(kcat_ref)
        vcat_ref[...] = jnp.zeros_like(vcat_ref)
        t_row = lax.broadcasted_iota(jnp.int32, (CHUNK, CHUNK), 0).astype(F32)
        t_col = lax.broadcasted_iota(jnp.int32, (CHUNK, CHUNK), 1).astype(F32)
        diff = t_row - t_col
        causal = diff >= 0
        for hh in range(RET_HEADS):
            lg = LOG_DECAY[hh]
            dmask_ref[hh] = jnp.where(causal, jnp.exp(jnp.where(causal, diff, 0.0) * lg), 0.0)
            qdec_ref[hh] = jnp.exp((t_row + 1.0) * lg)
            kdec_ref[hh] = jnp.exp((CHUNK - 1.0 - t_row) * lg)

    x = x_ref[...]
    xn = _rms_norm(x, g1_ref[...]).astype(BF16)
    proj_ref[...] = jnp.dot(xn, pltpu.bitcast(win_ref[...], BF16), preferred_element_type=F32)

    def chunk_body(c, carry):
        r0 = pl.multiple_of(c * CHUNK, CHUNK)
        rows = pl.ds(r0, CHUNK)
        chunk_idx = s * n_chunks + c
        t_local = lax.broadcasted_iota(jnp.int32, (CHUNK, 1), 0)
        pos = (chunk_idx * CHUNK + t_local).astype(F32)
        cos, sin_signed, even = _rotation_tables(pos, inv_ref[...])

        ret_heads = range(RET_HEADS)
        swa_heads = range(SWA_Q_HEADS)

        col = lambda off, hh: slice(off + hh * 128, off + (hh + 1) * 128)
        k_ret = [_rotate_pairs(proj_ref[rows, col(OFF_RK, hh)], cos, sin_signed, even)
                 * (RET_DK ** -0.5) for hh in ret_heads]
        qb = [_rotate_pairs(proj_ref[rows, col(OFF_RQ, hh)], cos, sin_signed, even).astype(BF16)
              for hh in ret_heads]
        kb = [k.astype(BF16) for k in k_ret]
        k_dec = [(k_ret[hh] * kdec_ref[hh]).astype(BF16) for hh in ret_heads]
        vb = [proj_ref[rows, col(OFF_RV, hh)].astype(BF16) for hh in ret_heads]
        kc = proj_ref[rows, OFF_SK:OFF_SK + 128]
        vc = proj_ref[rows, OFF_SV:OFF_SV + 128]
        kp_ref[...] = kc
        vp_ref[...] = vc
        lane = lax.broadcasted_iota(jnp.int32, (CHUNK, LANES), 1)
        lo = lane < 64
        for idx, (kvar, vvar) in enumerate(zip(_head_half_variants(kc, lo),
                                                _head_half_variants(vc, lo))):
            kcat_ref[idx, CHUNK:2 * CHUNK, :] = kvar.astype(BF16)
            vcat_ref[idx, CHUNK:2 * CHUNK, :] = vvar.astype(BF16)
        q_swa = [proj_ref[rows, col(OFF_SQ, i)].astype(BF16) for i in range(SWA_Q_HEADS // 2)]
        kv_idx = lambda head: (head // 4) * 2 + head % 2

        ret_scores = [lax.dot_general(qb[hh], kb[hh], NT_DIMS, preferred_element_type=F32)
                      for hh in ret_heads]
        ret_cross = [jnp.dot(qb[hh], state_ref[hh].astype(BF16), preferred_element_type=F32)
                     for hh in ret_heads]
        ret_kv = [lax.dot_general(k_dec[hh], vb[hh], TN_DIMS, preferred_element_type=F32)
                  for hh in ret_heads]
        swa_scores = [lax.dot_general(q_swa[head // 2], kcat_ref[kv_idx(head)], NT_DIMS,
                                      preferred_element_type=F32) for head in swa_heads]

        for hh in ret_heads:
            state_ref[hh] = state_ref[hh] * math.exp(CHUNK * LOG_DECAY[hh]) + ret_kv[hh]
        ret_p = [(ret_scores[hh] * dmask_ref[hh]).astype(BF16) for hh in ret_heads]
        q_row = lax.broadcasted_iota(jnp.int32, (CHUNK, 2 * CHUNK), 0)
        k_col = lax.broadcasted_iota(jnp.int32, (CHUNK, 2 * CHUNK), 1)
        valid = (k_col >= q_row + 1) & (k_col <= q_row + WINDOW)
        valid = valid & ((k_col >= CHUNK) | (chunk_idx > 0))
        swa_p, swa_den = [], []
        for head in swa_heads:
            sink = sinks_ref[head:head + 1, :]
            sc = jnp.where(valid, swa_scores[head] * (SWA_HD ** -0.5), NEG_INF)
            m = jnp.maximum(jnp.max(sc, axis=-1, keepdims=True), sink)
            p = jnp.exp(sc - m)
            swa_den.append(jnp.sum(p, axis=-1, keepdims=True) + jnp.exp(sink - m))
            swa_p.append(p.astype(BF16))

        ret_o = [jnp.dot(ret_p[hh], vb[hh], preferred_element_type=F32) for hh in ret_heads]
        swa_o = [jnp.dot(swa_p[head], vcat_ref[kv_idx(head)], preferred_element_type=F32)
                 for head in swa_heads]

        for hh in ret_heads:
            o = ret_o[hh] + ret_cross[hh] * qdec_ref[hh]
            o = o * lax.rsqrt(jnp.mean(o * o, axis=-1, keepdims=True) + RMS_EPS)
            gate = proj_ref[rows, col(OFF_RG, hh)]
            mix_ref[rows, col(0, hh)] = (o * _silu(gate)).astype(BF16)
        for i in range(SWA_Q_HEADS // 2):
            pair = swa_o[2 * i] / swa_den[2 * i] + swa_o[2 * i + 1] / swa_den[2 * i + 1]
            mix_ref[rows, col(512, i)] = pair.astype(BF16)
        for idx in range(4):
            kcat_ref[idx, 0:CHUNK, :] = kcat_ref[idx, CHUNK:2 * CHUNK, :]
            vcat_ref[idx, 0:CHUNK, :] = vcat_ref[idx, CHUNK:2 * CHUNK, :]
        return carry

    lax.fori_loop(0, n_chunks, chunk_body, 0)

    h = x + jnp.dot(mix_ref[...], pltpu.bitcast(wout_ref[...], BF16),
                             preferred_element_type=F32)
    h_ref[...] = h
    h2t_ref[...] = pltpu.bitcast(_rms_norm(h, g2_ref[...]).T.astype(BF16), jnp.uint32)
    st_ref[...] = state_ref[...]


def _prompt_mixer(x, g1, win, sinks, wout, g2, inv, *, tq):
    batch, seq, d = x.shape
    ns = seq // tq
    full = lambda shape: pl.BlockSpec(shape, lambda b, s: (0,) * len(shape))
    return pl.pallas_call(
        functools.partial(_prompt_mixer_kernel, tq=tq),
        grid=(batch, ns),
        in_specs=[
            pl.BlockSpec((None, tq, d), lambda b, s: (b, s, 0)),
            full((1, d)), full((d // 2, D_IN)), full((SWA_Q_HEADS, 1)), full((d // 2, d)),
            full((1, d)),
            full((1, LANES)),
        ],
        out_specs=[
            pl.BlockSpec((tq, d), lambda b, s: (b * ns + s, 0)),
            pl.BlockSpec((d // 2, tq), lambda b, s: (0, b * ns + s)),
            pl.BlockSpec((None, RET_HEADS, 128, 128), lambda b, s: (b, 0, 0, 0)),
            pl.BlockSpec((None, WINDOW, 128), lambda b, s: (b, 0, 0)),
            pl.BlockSpec((None, WINDOW, 128), lambda b, s: (b, 0, 0)),
        ],
        out_shape=[
            jax.ShapeDtypeStruct((batch * seq, d), F32),
            jax.ShapeDtypeStruct((d // 2, batch * seq), jnp.uint32),
            jax.ShapeDtypeStruct((batch, RET_HEADS, 128, 128), F32),
            jax.ShapeDtypeStruct((batch, WINDOW, 128), F32),
            jax.ShapeDtypeStruct((batch, WINDOW, 128), F32),
        ],
        scratch_shapes=[
            pltpu.VMEM((tq, D_IN), F32),
            pltpu.VMEM((RET_HEADS, 128, 128), F32),
            pltpu.VMEM((4, 2 * CHUNK, LANES), BF16),
            pltpu.VMEM((4, 2 * CHUNK, LANES), BF16),
            pltpu.VMEM((tq, d), BF16),
            pltpu.VMEM((RET_HEADS, CHUNK, CHUNK), F32),
            pltpu.VMEM((RET_HEADS, CHUNK, CHUNK), F32),
            pltpu.VMEM((RET_HEADS, CHUNK, CHUNK), F32),
        ],
        compiler_params=pltpu.CompilerParams(
            dimension_semantics=("arbitrary", "arbitrary"),
            vmem_limit_bytes=VMEM_LIMIT_BYTES),
        name="prompt_mixer",
    )(x, g1, win, sinks, wout, g2, inv)


def _sample_proj_kernel(x_ref, g1_ref, win_ref, inv_ref, proj_ref):
    xn = _rms_norm(x_ref[...], g1_ref[...]).astype(BF16)
    proj = jnp.dot(xn, pltpu.bitcast(win_ref[...], BF16), preferred_element_type=F32)
    proj_ref[...] = proj
    pos = jnp.full((1, 1), float(PAST_LEN), F32)
    cos, sin_signed, even = _rotation_tables(pos * jnp.ones((1, LANES), F32), inv_ref[...])
    even_full = jnp.broadcast_to(even, (x_ref.shape[0], LANES))
    for hh in range(RET_HEADS):
        ql = slice(OFF_RQ + hh * 128, OFF_RQ + (hh + 1) * 128)
        kl = slice(OFF_RK + hh * 128, OFF_RK + (hh + 1) * 128)
        proj_ref[:, ql] = _rotate_pairs(proj[:, ql], cos, sin_signed, even_full)
        proj_ref[:, kl] = _rotate_pairs(proj[:, kl], cos, sin_signed, even_full) * (RET_DK ** -0.5)


def _sample_proj(x, g1, win, inv):
    n = x.shape[0]
    return pl.pallas_call(
        _sample_proj_kernel,
        out_shape=jax.ShapeDtypeStruct((n, D_IN), F32),
        compiler_params=pltpu.CompilerParams(vmem_limit_bytes=VMEM_LIMIT_BYTES),
        name="sample_proj",
    )(x, g1, win, inv)


def _sample_state_kernel(proj_ref, st_ref, ck_ref, cv_ref, sinks_ref,
                         stn_ref, kn_ref, vn_ref, o2_ref, sres_ref, *, group):
    row8 = lax.broadcasted_iota(jnp.int32, (8, LANES), 0)
    lo1 = lax.broadcasted_iota(jnp.int32, (1, LANES), 1) < 64
    sink_col = sinks_ref[...]

    def sample_body(b, carry):
        row = pl.ds(b, 1)
        for hh in range(RET_HEADS):
            gamma = math.exp(LOG_DECAY[hh])
            q = proj_ref[row, OFF_RQ + hh * 128:OFF_RQ + (hh + 1) * 128]
            k = proj_ref[row, OFF_RK + hh * 128:OFF_RK + (hh + 1) * 128]
            v = proj_ref[row, OFF_RV + hh * 128:OFF_RV + (hh + 1) * 128]
            state = st_ref[b, hh]
            q8 = jnp.broadcast_to(q, (8, LANES)).astype(BF16)
            o2 = jnp.dot(q8, state.astype(BF16), preferred_element_type=F32)
            o2_ref[row, hh * 128:(hh + 1) * 128] = o2[0:1, :]
            k8 = jnp.where(row8 == 0, jnp.broadcast_to(k, (8, LANES)), 0.0).astype(BF16)
            v8 = jnp.broadcast_to(v, (8, LANES)).astype(BF16)
            outer = lax.dot_general(k8, v8, TN_DIMS, preferred_element_type=F32)
            stn_ref[b, hh] = state * gamma + outer
        kn_ref[b, 0:WINDOW - 1, :] = ck_ref[b, 1:WINDOW, :]
        vn_ref[b, 0:WINDOW - 1, :] = cv_ref[b, 1:WINDOW, :]
        kn_ref[b, WINDOW - 1:WINDOW, :] = proj_ref[row, OFF_SK:OFF_SK + 128]
        vn_ref[b, WINDOW - 1:WINDOW, :] = proj_ref[row, OFF_SV:OFF_SV + 128]
        kwin = kn_ref[b].astype(BF16)
        vwin = vn_ref[b].astype(BF16)
        q_pad = jnp.zeros((8, LANES), F32)
        for qb_idx in range(SWA_Q_HEADS // 2):
            g = qb_idx // 2
            blk = proj_ref[row, OFF_SQ + qb_idx * 128:OFF_SQ + (qb_idx + 1) * 128]
            rolled = pltpu.roll(blk, 64, 1)
            zero = jnp.zeros_like(blk)
            if g == 0:
                first, second = jnp.where(lo1, blk, zero), jnp.where(lo1, rolled, zero)
            else:
                first, second = jnp.where(lo1, zero, rolled), jnp.where(lo1, zero, blk)
            q_pad = jnp.where(row8 == 2 * qb_idx, jnp.broadcast_to(first, (8, LANES)), q_pad)
            q_pad = jnp.where(row8 == 2 * qb_idx + 1, jnp.broadcast_to(second, (8, LANES)), q_pad)
        sc = lax.dot_general(q_pad.astype(BF16), kwin, NT_DIMS,
                             preferred_element_type=F32) * (SWA_HD ** -0.5)
        m = jnp.maximum(jnp.max(sc, axis=-1, keepdims=True), sink_col)
        p = jnp.exp(sc - m)
        den = jnp.sum(p, axis=-1, keepdims=True) + jnp.exp(sink_col - m)
        sres_ref[b] = jnp.dot(p.astype(BF16), vwin, preferred_element_type=F32) / den
        return carry

    for b in range(group):
        sample_body(b, 0)


def _sample_state(proj, state, ck, cv, sinks, *, group):
    n = proj.shape[0]
    return pl.pallas_call(
        functools.partial(_sample_state_kernel, group=group),
        grid=(n // group,),
        in_specs=[
            pl.BlockSpec((group, D_IN), lambda i: (i, 0)),
            pl.BlockSpec((group, RET_HEADS, 128, 128), lambda i: (i, 0, 0, 0)),
            pl.BlockSpec((group, WINDOW, 128), lambda i: (i, 0, 0)),
            pl.BlockSpec((group, WINDOW, 128), lambda i: (i, 0, 0)),
            pl.BlockSpec((SWA_Q_HEADS, 1), lambda i: (0, 0)),
        ],
        out_specs=[
            pl.BlockSpec((group, RET_HEADS, 128, 128), lambda i: (i, 0, 0, 0)),
            pl.BlockSpec((group, WINDOW, 128), lambda i: (i, 0, 0)),
            pl.BlockSpec((group, WINDOW, 128), lambda i: (i, 0, 0)),
            pl.BlockSpec((group, 512), lambda i: (i, 0)),
            pl.BlockSpec((group, SWA_Q_HEADS, 128), lambda i: (i, 0, 0)),
        ],
        out_shape=[
            jax.ShapeDtypeStruct((n, RET_HEADS, 128, 128), F32),
            jax.ShapeDtypeStruct((n, WINDOW, 128), F32),
            jax.ShapeDtypeStruct((n, WINDOW, 128), F32),
            jax.ShapeDtypeStruct((n, 512), F32),
            jax.ShapeDtypeStruct((n, SWA_Q_HEADS, 128), F32),
        ],
        compiler_params=pltpu.CompilerParams(
            dimension_semantics=("arbitrary",), vmem_limit_bytes=VMEM_LIMIT_BYTES),
        name="sample_state",
    )(proj, state, ck, cv, sinks)


def _sample_finish_kernel(x_ref, proj_ref, o2_ref, sres_ref, wout_ref, g2_ref,
                          h_ref, h2t_ref, mix_ref):
    n = x_ref.shape[0]
    for hh in range(RET_HEADS):
        gamma = math.exp(LOG_DECAY[hh])
        lanes = slice(hh * 128, (hh + 1) * 128)
        q = proj_ref[:, OFF_RQ + hh * 128:OFF_RQ + (hh + 1) * 128]
        k = proj_ref[:, OFF_RK + hh * 128:OFF_RK + (hh + 1) * 128]
        v = proj_ref[:, OFF_RV + hh * 128:OFF_RV + (hh + 1) * 128]
        gate = proj_ref[:, OFF_RG + hh * 128:OFF_RG + (hh + 1) * 128]
        qk = jnp.sum(q * k, axis=-1, keepdims=True)
        o = qk * v + o2_ref[:, lanes] * gamma
        o = o * lax.rsqrt(jnp.mean(o * o, axis=-1, keepdims=True) + RMS_EPS)
        mix_ref[:, lanes] = (o * _silu(gate)).astype(BF16)
    lo = lax.broadcasted_iota(jnp.int32, (n, LANES), 1) < 64
    for qb_idx in range(SWA_Q_HEADS // 2):
        r0 = sres_ref[:, (2 * qb_idx) * 128:(2 * qb_idx + 1) * 128]
        r1 = sres_ref[:, (2 * qb_idx + 1) * 128:(2 * qb_idx + 2) * 128]
        if qb_idx // 2 == 0:
            blk = jnp.where(lo, r0, pltpu.roll(r1, 64, 1))
        else:
            blk = jnp.where(lo, pltpu.roll(r0, 64, 1), r1)
        mix_ref[:, 512 + qb_idx * 128:512 + (qb_idx + 1) * 128] = blk.astype(BF16)
    h = x_ref[...] + jnp.dot(mix_ref[...], pltpu.bitcast(wout_ref[...], BF16),
                             preferred_element_type=F32)
    h_ref[...] = h
    h2t_ref[...] = pltpu.bitcast(_rms_norm(h, g2_ref[...]).T.astype(BF16), jnp.uint32)


def _sample_finish(x, proj, o2, sres, wout, g2):
    n, d = x.shape
    return pl.pallas_call(
        _sample_finish_kernel,
        out_shape=[jax.ShapeDtypeStruct((n, d), F32),
                   jax.ShapeDtypeStruct((d // 2, n), jnp.uint32)],
        scratch_shapes=[pltpu.VMEM((n, d), BF16)],
        compiler_params=pltpu.CompilerParams(vmem_limit_bytes=VMEM_LIMIT_BYTES),
        name="sample_finish",
    )(x, proj, o2, sres, wout, g2)


def _sorting_network(n):
    def merge(lo, hi, r):
        step = r * 2
        if step < hi - lo:
            yield from merge(lo, hi, step)
            yield from merge(lo + r, hi, step)
            yield from [(i, i + r) for i in range(lo + r, hi - r, step)]
        else:
            yield (lo, lo + r)

    def sort(lo, hi):
        if hi - lo >= 1:
            mid = lo + (hi - lo) // 2
            yield from sort(lo, mid)
            yield from sort(mid + 1, hi)
            yield from merge(lo, hi, 1)

    size = 1
    while size < n:
        size *= 2
    return tuple((i, j) for i, j in sort(0, size - 1) if j < n)


def _top_values(lists, count):
    lists = list(lists)
    for i, j in _sorting_network(len(lists)):
        lists[i], lists[j] = jnp.maximum(lists[i], lists[j]), jnp.minimum(lists[i], lists[j])
    rows = []
    for p in range(count):
        m = jnp.max(lists[0], axis=0, keepdims=True)
        rows.append(m)
        if p + 1 < count:
            hit = lists[0] == m
            depth = min(len(lists), count - p)
            for k in range(depth - 1):
                lists[k] = jnp.where(hit, lists[k + 1], lists[k])
            if depth == len(lists):
                lists[depth - 1] = jnp.where(hit, NEG_INF, lists[depth - 1])
    return rows


def _sublane_lists(s):
    return [s[k * 8:(k + 1) * 8, :] for k in range(s.shape[0] // 8)]


def _stack8(rows, row8):
    out = jnp.zeros(row8.shape, F32)
    for r, v in enumerate(rows):
        out = jnp.where(row8 == r, jnp.broadcast_to(v, row8.shape), out)
    return out


def _routing_tables(s0, s1):
    a = _top_values(_sublane_lists(s0), PEER_TOPK)
    b = _top_values(_sublane_lists(s1), PEER_TOPK)
    row8 = lax.broadcasted_iota(jnp.int32, (8, s0.shape[1]), 0)
    a_lo, a_hi = _stack8(a[:8], row8), _stack8(a[8:], row8)
    b_lo, b_hi = _stack8(b[:8], row8), _stack8(b[8:], row8)
    neg = jnp.full(row8.shape, NEG_INF, F32)
    cands = [
        a[0] + b_lo, a[0] + b_hi, a[1] + b_lo,
        jnp.where(row8 < 5, a[2] + b_lo, neg),
        jnp.where(row8 < 4, a[3] + b_lo, neg),
        b[0] + a_hi,
        jnp.where(row8 >= 4, b[0] + a_lo, neg),
        jnp.where(row8 >= 4, b[1] + a_lo, neg),
        jnp.where(row8 == 4, b[2] + a_lo, neg),
    ]
    tau = _top_values(cands, PEER_TOPK)[-1]
    top = a[0] + b[0]
    passed = [c >= tau for c in cands]
    col_sum = lambda x: jnp.sum(x, axis=0, keepdims=True)
    z = functools.reduce(lambda x, y: x + y, [
        col_sum(jnp.where(ok, jnp.exp(c - top), 0.0)) for ok, c in zip(passed, cands)])
    cnt = [jnp.where(ok, 1.0, 0.0) for ok in passed]
    by_row = cnt[6] + cnt[7] + cnt[8]
    n = [col_sum(cnt[0]) + col_sum(cnt[1]), col_sum(cnt[2]), col_sum(cnt[3]), col_sum(cnt[4])]
    n += [by_row[p:p + 1, :] for p in range(4, 8)]
    n += [cnt[5][p:p + 1, :] for p in range(8)]
    n0 = jnp.zeros(s0.shape, F32)
    rank1 = jnp.zeros(s1.shape, F32)
    for p in range(PEER_TOPK):
        n0 = jnp.where(s0 == a[p], n[p], n0)
        rank1 = jnp.where(s1 < b[p], p + 1.0, rank1)
    e1 = jnp.exp(s1 - b[0])
    e0n = jnp.exp(s0 - a[0]) * (1.0 / z)
    return rank1, e1, n0, e0n


def _routing_kernel(h2t_ref, wqt_ref, keys_ref, r1_ref, e1_ref, n0_ref, e0_ref,
                    q_ref, s_ref, *, tt):
    q_ref[...] = jnp.dot(pltpu.bitcast(wqt_ref[...], BF16), pltpu.bitcast(h2t_ref[...], BF16),
                         preferred_element_type=F32)
    n_lg = tt // LANES

    def head_body(h, carry):
        for c in range(2):
            r0 = pl.multiple_of((2 * h + c) * 128, 128)
            q_hc = q_ref[pl.ds(r0, 128), :].astype(BF16)
            key_words = keys_ref[pl.ds(pl.multiple_of((2 * h + c) * KEY_WORDS, KEY_WORDS),
                                       KEY_WORDS), :]
            s_ref[c] = jnp.dot(pltpu.bitcast(key_words, BF16), q_hc, preferred_element_type=F32)
        out_rows = pl.ds(pl.multiple_of(h * 128, 128), 128)
        packed_rows = pl.ds(pl.multiple_of(h * KEY_WORDS, KEY_WORDS), KEY_WORDS)
        rank1, e1, n0, e0n = _routing_tables(s_ref[0], s_ref[1])
        for lg in range(n_lg):
            lanes = slice(lg * LANES, (lg + 1) * LANES)
            r1_ref[lg, packed_rows, :] = pltpu.bitcast(rank1[:, lanes].astype(BF16), jnp.uint32)
            e1_ref[lg, packed_rows, :] = pltpu.bitcast(e1[:, lanes].astype(BF16), jnp.uint32)
            n0_ref[lg, out_rows, :] = n0[:, lanes]
            e0_ref[lg, out_rows, :] = e0n[:, lanes]
        return carry

    lax.fori_loop(0, PEER_HEADS, head_body, 0)


def _routing(h2t, wqt, keys, *, tt):
    d, t = 2 * h2t.shape[0], h2t.shape[1]
    rows = PEER_HEADS * PEER_N_KEYS
    tab = lambda r, dt: jax.ShapeDtypeStruct((t // LANES, r, LANES), dt)
    tab_spec = lambda r: pl.BlockSpec((tt // LANES, r, LANES), lambda i: (i, 0, 0))
    return pl.pallas_call(
        functools.partial(_routing_kernel, tt=tt),
        grid=(t // tt,),
        in_specs=[
            pl.BlockSpec((d // 2, tt), lambda i: (0, i)),
            pl.BlockSpec(wqt.shape, lambda i: (0, 0)),
            pl.BlockSpec(keys.shape, lambda i: (0, 0)),
        ],
        out_specs=[tab_spec(rows // 2), tab_spec(rows // 2), tab_spec(rows), tab_spec(rows)],
        out_shape=[tab(rows // 2, jnp.uint32), tab(rows // 2, jnp.uint32),
                   tab(rows, F32), tab(rows, F32)],
        scratch_shapes=[pltpu.VMEM((2 * wqt.shape[0], tt), F32), pltpu.VMEM((2, 128, tt), F32)],
        compiler_params=pltpu.CompilerParams(
            dimension_semantics=("arbitrary",), vmem_limit_bytes=VMEM_LIMIT_BYTES),
        name="peer_routing",
    )(h2t, wqt, keys)


def _experts_kernel(xt_ref, u_ref, vt_ref, r1_ref, e1_ref, n0_ref, e0_ref, h_ref, fg_ref,
                    y_ref, acc_ref, hd_ref, at_ref, *, tt, et, n_e, n_tiles):
    g = pl.program_id(0)
    e_c = jnp.clip(g - 2, 0, n_tiles - 1) % n_e

    @pl.when(g == 0)
    def _prime():
        hd_ref[...] = jnp.zeros_like(hd_ref)
        at_ref[...] = jnp.zeros_like(at_ref)

    @pl.when(e_c == 0)
    def _new_token_tile():
        acc_ref[...] = jnp.zeros_like(acc_ref)

    stages = functools.partial(_experts_stages, xt_ref, u_ref, vt_ref, r1_ref, e1_ref, n0_ref,
                               e0_ref, acc_ref, hd_ref, at_ref, tt=tt, et=et, n_e=n_e,
                               n_tiles=n_tiles)
    pl.when(g % 2 == 0)(functools.partial(stages, slot_a=0))
    pl.when(g % 2 == 1)(functools.partial(stages, slot_a=1))

    @pl.when((g >= 2) & (e_c == n_e - 1))
    def _finish():
        y_ref[...] = _rms_norm(h_ref[...] + acc_ref[...].T, fg_ref[...])


def _experts_stages(xt_ref, u_ref, vt_ref, r1_ref, e1_ref, n0_ref, e0_ref, acc_ref, hd_ref,
                    at_ref, *, tt, et, n_e, n_tiles, slot_a):
    slot_b = 1 - slot_a
    n_ib = et // PEER_N_KEYS
    sub = 16
    e_b = jnp.clip(pl.program_id(0) - 1, 0, n_tiles - 1) % n_e

    def packed(rows):
        return slice(rows.start // 2, rows.stop // 2)

    def stage_a(rows):
        u = pltpu.bitcast(u_ref[packed(rows), :], BF16)
        xt = pltpu.bitcast(xt_ref[...], BF16)
        hd_ref[slot_a, rows, :] = jnp.dot(u, xt, preferred_element_type=F32)

    def stage_b(ii, lg):
        lanes = slice(lg * LANES, (lg + 1) * LANES)
        rows = slice(ii * 128, (ii + 1) * 128)
        w = jnp.zeros((PEER_N_KEYS, LANES), BF16)
        for h in range(PEER_HEADS):
            grp = pl.ds(pl.multiple_of(h * 128 + e_b * n_ib + (ii // 8) * 8, 8), 8)
            n0b = jnp.broadcast_to(n0_ref[lg, grp, :][ii % 8:ii % 8 + 1, :], (sub, LANES))
            e0b = jnp.broadcast_to(e0_ref[lg, grp, :][ii % 8:ii % 8 + 1, :], (sub, LANES))
            n0b = jnp.concatenate([n0b.astype(BF16)] * (PEER_N_KEYS // sub), axis=0)
            e0b = jnp.concatenate([e0b.astype(BF16)] * (PEER_N_KEYS // sub), axis=0)
            r1 = pltpu.bitcast(r1_ref[lg, h * KEY_WORDS:(h + 1) * KEY_WORDS, :], BF16)
            e1 = pltpu.bitcast(e1_ref[lg, h * KEY_WORDS:(h + 1) * KEY_WORDS, :], BF16)
            w = w + jnp.where(r1 < n0b, e1 * e0b, jnp.zeros_like(e1))
        x = hd_ref[slot_b, rows, lanes]
        gelu = 0.5 * x * (1.0 + lax.erf(x * math.sqrt(0.5)))
        at_ref[slot_b, rows, lanes] = gelu.astype(BF16) * w

    def stage_c(rows):
        vt = pltpu.bitcast(vt_ref[packed(rows), :], BF16)
        acc_ref[rows, :] += jnp.dot(vt, at_ref[slot_a], preferred_element_type=F32)

    d = acc_ref.shape[0]
    n_lg = tt // LANES
    b_pieces = [(ii, lg) for ii in range(n_ib) for lg in range(n_lg)]
    pieces = sorted([((k + 0.5) * MXU_ROWS / et, 0, k) for k in range(et // MXU_ROWS)]
                    + [((k + 0.5) * MXU_ROWS / d, 1, k) for k in range(d // MXU_ROWS)])
    done = 0
    for idx, (_, is_c, k) in enumerate(pieces):
        upto = int((idx + 0.5) * len(b_pieces) / len(pieces))
        for piece in b_pieces[done:upto]:
            stage_b(*piece)
        done = max(done, upto)
        (stage_c if is_c else stage_a)(slice(k * MXU_ROWS, (k + 1) * MXU_ROWS))
    for piece in b_pieces[done:]:
        stage_b(*piece)


def _experts(xt, u, vt, r1, e1, n0, e0, h, fg, *, tt, et):
    d, t = 2 * xt.shape[0], xt.shape[1]
    n_exp = 2 * u.shape[0]
    rows = n0.shape[1]
    assert et % (8 * PEER_N_KEYS) == 0 and t % tt == 0 and n_exp % et == 0
    n_e = n_exp // et
    n_tiles = (t // tt) * n_e
    tile_a = lambda g: jnp.minimum(g, n_tiles - 1)
    tile_b = lambda g: jnp.clip(g - 1, 0, n_tiles - 1)
    tile_c = lambda g: jnp.clip(g - 2, 0, n_tiles - 1)
    tab_spec = lambda r: pl.BlockSpec((tt // LANES, r, LANES), lambda g: (tile_b(g) // n_e, 0, 0))
    return pl.pallas_call(
        functools.partial(_experts_kernel, tt=tt, et=et, n_e=n_e, n_tiles=n_tiles),
        grid=(n_tiles + 2,),
        in_specs=[
            pl.BlockSpec((d // 2, tt), lambda g: (0, tile_a(g) // n_e)),
            pl.BlockSpec((et // 2, d), lambda g: (tile_a(g) % n_e, 0)),
            pl.BlockSpec((d // 2, et), lambda g: (0, tile_c(g) % n_e)),
            tab_spec(rows // 2), tab_spec(rows // 2), tab_spec(rows), tab_spec(rows),
            pl.BlockSpec((tt, d), lambda g: (tile_c(g) // n_e, 0)),
            pl.BlockSpec((1, d), lambda g: (0, 0)),
        ],
        out_specs=pl.BlockSpec((tt, d), lambda g: (tile_c(g) // n_e, 0)),
        out_shape=jax.ShapeDtypeStruct((t, d), F32),
        scratch_shapes=[
            pltpu.VMEM((d, tt), F32),
            pltpu.VMEM((2, et, tt), F32),
            pltpu.VMEM((2, et, tt), BF16),
        ],
        compiler_params=pltpu.CompilerParams(
            dimension_semantics=("arbitrary",),
            vmem_limit_bytes=VMEM_LIMIT_BYTES),
        name="peer_experts",
    )(xt, u, vt, r1, e1, n0, e0, h, fg)


def _pack_kernel(x_ref, o_ref, *, transpose):
    x = x_ref[...]
    if transpose:
        x = x.T
    o_ref[...] = pltpu.bitcast(x.astype(BF16), jnp.uint32)


def _pack_bf16(x, *, transpose=False, block_rows=1024):
    m, n = x.shape
    bm = min(block_rows, m)
    assert m % bm == 0
    if transpose:
        out_shape, out_spec = (n // 2, m), pl.BlockSpec((n // 2, bm), lambda i: (0, i))
    else:
        out_shape, out_spec = (m // 2, n), pl.BlockSpec((bm // 2, n), lambda i: (i, 0))
    return pl.pallas_call(
        functools.partial(_pack_kernel, transpose=transpose),
        grid=(m // bm,),
        in_specs=[pl.BlockSpec((bm, n), lambda i: (i, 0))],
        out_specs=out_spec,
        out_shape=jax.ShapeDtypeStruct(out_shape, jnp.uint32),
        compiler_params=pltpu.CompilerParams(
            dimension_semantics=("arbitrary",), vmem_limit_bytes=VMEM_LIMIT_BYTES),
        name="pack_bf16",
    )(x)


def _peer_and_final(h, h2t, wqt, keys, u, vt, fg, *, tt_route, tt, et):
    r1, e1, n0, e0 = _routing(h2t, wqt, keys, tt=tt_route)
    return _experts(h2t, u, vt, r1, e1, n0, e0, h, fg, tt=tt, et=et)


def kernel(x_prompt, x_sample, state_ret, cache_swa_k, cache_swa_v, norm1_g, w_in, swa_sinks, w_out, norm2_g, peer_w_q, peer_sub_keys, peer_u, peer_v, final_g):
    batch, seq, d = x_prompt.shape
    n_s = x_sample.shape[0]
    g1 = norm1_g.reshape(1, d)
    g2 = norm2_g.reshape(1, d)
    fg = final_g.reshape(1, d)
    sinks = swa_sinks.astype(F32).reshape(SWA_Q_HEADS, 1)
    win = _pack_bf16(w_in)
    wout = _pack_bf16(w_out)
    wqt = _pack_bf16(peer_w_q, transpose=True)
    keys = _pack_bf16(peer_sub_keys.reshape(PEER_HEADS * 2 * PEER_N_KEYS, -1))
    u = _pack_bf16(peer_u)
    vt = _pack_bf16(peer_v, transpose=True)
    half = RET_DK // 2
    inv = 1.0 / (ROPE_BASE ** jnp.linspace(0.0, 1.0, half, dtype=F32))
    inv = jnp.repeat(inv, 2).reshape(1, RET_DK)

    h_p, h2t_p, st_p, k_p, v_p = _prompt_mixer(x_prompt, g1, win, sinks, wout, g2, inv,
                                               tq=PROMPT_MIXER_TOKENS)
    y_p = _peer_and_final(h_p, h2t_p, wqt, keys, u, vt, fg, tt_route=PROMPT_ROUTING_TOKENS,
                          tt=PROMPT_EXPERT_TOKENS, et=PROMPT_EXPERTS_PER_STEP)

    xs = x_sample.reshape(n_s, d)
    proj = _sample_proj(xs, g1, win, inv)
    st_s, k_s, v_s, o2, sres = _sample_state(
        proj, state_ret, cache_swa_k.reshape(n_s, WINDOW, 128),
        cache_swa_v.reshape(n_s, WINDOW, 128), sinks, group=SAMPLE_STATE_GROUP)
    h_s, h2t_s = _sample_finish(xs, proj, o2, sres.reshape(n_s, SWA_Q_HEADS * 128), wout, g2)
    y_s = _peer_and_final(h_s, h2t_s, wqt, keys, u, vt, fg, tt_route=n_s, tt=n_s,
                          et=SAMPLE_EXPERTS_PER_STEP)

    kv_shape = (WINDOW, 2, SWA_HD)
    return (y_p.reshape(batch, seq, d), y_s.reshape(n_s, 1, d), st_p,
            k_p.reshape(batch, *kv_shape), v_p.reshape(batch, *kv_shape),
            st_s, k_s.reshape(n_s, *kv_shape), v_s.reshape(n_s, *kv_shape))
```

```python
import functools
import math

import jax
import jax.numpy as jnp
import numpy as np
from jax import lax
from jax.experimental import pallas as pl
from jax.experimental.pallas import tpu as pltpu

F32 = jnp.float32
BF16 = jnp.bfloat16

D_MODEL = 1024
SEQ = 8192
PAST_LEN = 8192
RET_HEADS = 4
RET_DK = 128
CHUNK = 128
ROPE_BASE = 10000.0
SWA_Q_HEADS = 8
SWA_HD = 64
WINDOW = 128
D_IN = 2816
OFF_RQ, OFF_RK, OFF_RV, OFF_RG, OFF_SQ, OFF_SK, OFF_SV = 0, 512, 1024, 1536, 2048, 2560, 2688
PEER_HEADS = 8
PEER_N_KEYS = 128
PEER_N_EXPERTS = PEER_N_KEYS * PEER_N_KEYS
PEER_TOPK = 16
RMS_EPS = 1e-6
LANES = 128

LOG_DECAY = tuple(math.log(1.0 - 2.0 ** (-5.0 - h)) for h in range(RET_HEADS))

VMEM_LIMIT_BYTES = 60 * 1024 * 1024
MXU_ROWS = 256

PROMPT_MIXER_TOKENS = 512
PROMPT_ROUTING_TOKENS = 1024
PROMPT_EXPERT_TOKENS = 512
PROMPT_EXPERTS_PER_STEP = 2048
SAMPLE_EXPERTS_PER_STEP = 1024
SAMPLE_STATE_GROUP = 8
FIRST_KEYS_PER_PIECE = 2
KEY_WORDS = PEER_N_KEYS // 2

NEG_INF = float("-inf")
NT_DIMS = (((1,), (1,)), ((), ()))
TN_DIMS = (((0,), (0,)), ((), ()))


def _rms_norm(x, g):
    return x * lax.rsqrt(jnp.mean(x * x, axis=-1, keepdims=True) + RMS_EPS) * g


def _rotation_tables(pos, inv):
    ang = pos * inv
    cos = jnp.cos(ang)
    sin = jnp.sin(ang)
    lane = lax.broadcasted_iota(jnp.int32, ang.shape, 1)
    even = (lane & 1) == 0
    return cos, jnp.where(even, -sin, sin), even


def _rotate_pairs(x, cos, sin_signed, even):
    partner = jnp.where(even, pltpu.roll(x, LANES - 1, 1), pltpu.roll(x, 1, 1))
    return x * cos + partner * sin_signed


def _silu(g):
    return g * jax.nn.sigmoid(g)


def _head_half_variants(a, lo):
    rolled = pltpu.roll(a, 64, 1)
    zero = jnp.zeros_like(a)
    return (jnp.where(lo, a, zero), jnp.where(lo, zero, rolled),
            jnp.where(lo, rolled, zero), jnp.where(lo, zero, a))


def _prompt_mixer_kernel(x_ref, g1_ref, win_ref, sinks_ref, wout_ref, g2_ref, inv_ref,
                         h_ref, h2t_ref, st_ref, kp_ref, vp_ref,
                         proj_ref, state_ref, kcat_ref, vcat_ref, mix_ref,
                         dmask_ref, qdec_ref, kdec_ref, *, tq):
    s = pl.program_id(1)
    n_chunks = tq // CHUNK

    @pl.when(s == 0)
    def _init():
        state_ref[...] = jnp.zeros_like(state_ref)
        kcat_ref[...] = jnp.zeros_like(kcat_ref)
        vcat_ref[...] = jnp.zeros_like(vcat_ref)
        t_row = lax.broadcasted_iota(jnp.int32, (CHUNK, CHUNK), 0).astype(F32)
        t_col = lax.broadcasted_iota(jnp.int32, (CHUNK, CHUNK), 1).astype(F32)
        diff = t_row - t_col
        causal = diff >= 0
        for hh in range(RET_HEADS):
            lg = LOG_DECAY[hh]
            dmask_ref[hh] = jnp.where(causal, jnp.exp(jnp.where(causal, diff, 0.0) * lg), 0.0)
            qdec_ref[hh] = jnp.exp((t_row + 1.0) * lg)
            kdec_ref[hh] = jnp.exp((CHUNK - 1.0 - t_row) * lg)

    x = x_ref[...]
    xn = _rms_norm(x, g1_ref[...]).astype(BF16)
    proj_ref[...] = jnp.dot(xn, pltpu.bitcast(win_ref[...], BF16), preferred_element_type=F32)

    def chunk_body(c, carry):
        r0 = pl.multiple_of(c * CHUNK, CHUNK)
        rows = pl.ds(r0, CHUNK)
        chunk_idx = s * n_chunks + c
        t_local = lax.broadcasted_iota(jnp.int32, (CHUNK, 1), 0)
        pos = (chunk_idx * CHUNK + t_local).astype(F32)
        cos, sin_signed, even = _rotation_tables(pos, inv_ref[...])

        ret_heads = range(RET_HEADS)
        swa_heads = range(SWA_Q_HEADS)

        col = lambda off, hh: slice(off + hh * 128, off + (hh + 1) * 128)
        k_ret = [_rotate_pairs(proj_ref[rows, col(OFF_RK, hh)], cos, sin_signed, even)
                 * (RET_DK ** -0.5) for hh in ret_heads]
        qb = [_rotate_pairs(proj_ref[rows, col(OFF_RQ, hh)], cos, sin_signed, even).astype(BF16)
              for hh in ret_heads]
        kb = [k.astype(BF16) for k in k_ret]
        k_dec = [(k_ret[hh] * kdec_ref[hh]).astype(BF16) for hh in ret_heads]
        vb = [proj_ref[rows, col(OFF_RV, hh)].astype(BF16) for hh in ret_heads]
        kc = proj_ref[rows, OFF_SK:OFF_SK + 128]
        vc = proj_ref[rows, OFF_SV:OFF_SV + 128]
        kp_ref[...] = kc
        vp_ref[...] = vc
        lane = lax.broadcasted_iota(jnp.int32, (CHUNK, LANES), 1)
        lo = lane < 64
        for idx, (kvar, vvar) in enumerate(zip(_head_half_variants(kc, lo),
                                                _head_half_variants(vc, lo))):
            kcat_ref[idx, CHUNK:2 * CHUNK, :] = kvar.astype(BF16)
            vcat_ref[idx, CHUNK:2 * CHUNK, :] = vvar.astype(BF16)
        q_swa = [proj_ref[rows, col(OFF_SQ, i)].astype(BF16) for i in range(SWA_Q_HEADS // 2)]
        kv_idx = lambda head: (head // 4) * 2 + head % 2

        ret_scores = [lax.dot_general(qb[hh], kb[hh], NT_DIMS, preferred_element_type=F32)
                      for hh in ret_heads]
        ret_cross = [jnp.dot(qb[hh], state_ref[hh].astype(BF16), preferred_element_type=F32)
                     for hh in ret_heads]
        ret_kv = [lax.dot_general(k_dec[hh], vb[hh], TN_DIMS, preferred_element_type=F32)
                  for hh in ret_heads]
        swa_scores = [lax.dot_general(q_swa[head // 2], kcat_ref[kv_idx(head)], NT_DIMS,
                                      preferred_element_type=F32) for head in swa_heads]

        for hh in ret_heads:
            state_ref[hh] = state_ref[hh] * math.exp(CHUNK * LOG_DECAY[hh]) + ret_kv[hh]
        ret_p = [(ret_scores[hh] * dmask_ref[hh]).astype(BF16) for hh in ret_heads]
        q_row = lax.broadcasted_iota(jnp.int32, (CHUNK, 2 * CHUNK), 0)
        k_col = lax.broadcasted_iota(jnp.int32, (CHUNK, 2 * CHUNK), 1)
        valid = (k_col >= q_row + 1) & (k_col <= q_row + WINDOW)
        valid = valid & ((k_col >= CHUNK) | (chunk_idx > 0))
        swa_p, swa_den = [], []
        for head in swa_heads:
            sink = sinks_ref[head:head + 1, :]
            sc = jnp.where(valid, swa_scores[head] * (SWA_HD ** -0.5), NEG_INF)
            m = jnp.maximum(jnp.max(sc, axis=-1, keepdims=True), sink)
            p = jnp.exp(sc - m)
            swa_den.append(jnp.sum(p, axis=-1, keepdims=True) + jnp.exp(sink - m))
            swa_p.append(p.astype(BF16))

        ret_o = [jnp.dot(ret_p[hh], vb[hh], preferred_element_type=F32) for hh in ret_heads]
        swa_o = [jnp.dot(swa_p[head], vcat_ref[kv_idx(head)], preferred_element_type=F32)
                 for head in swa_heads]

        for hh in ret_heads:
            o = ret_o[hh] + ret_cross[hh] * qdec_ref[hh]
            o = o * lax.rsqrt(jnp.mean(o * o, axis=-1, keepdims=True) + RMS_EPS)
            gate = proj_ref[rows, col(OFF_RG, hh)]
            mix_ref[rows, col(0, hh)] = (o * _silu(gate)).astype(BF16)
        for i in range(SWA_Q_HEADS // 2):
            pair = swa_o[2 * i] / swa_den[2 * i] + swa_o[2 * i + 1] / swa_den[2 * i + 1]
            mix_ref[rows, col(512, i)] = pair.astype(BF16)
        for idx in range(4):
            kcat_ref[idx, 0:CHUNK, :] = kcat_ref[idx, CHUNK:2 * CHUNK, :]
            vcat_ref[idx, 0:CHUNK, :] = vcat_ref[idx, CHUNK:2 * CHUNK, :]
        return carry

    lax.fori_loop(0, n_chunks, chunk_body, 0)

    h = x + jnp.dot(mix_ref[...], pltpu.bitcast(wout_ref[...], BF16),
                             preferred_element_type=F32)
    h_ref[...] = h
    h2t_ref[...] = pltpu.bitcast(_rms_norm(h, g2_ref[...]).T.astype(BF16), jnp.uint32)
    st_ref[...] = state_ref[...]


def _prompt_mixer(x, g1, win, sinks, wout, g2, inv, *, tq):
    batch, seq, d = x.shape
    ns = seq // tq
    full = lambda shape: pl.BlockSpec(shape, lambda b, s: (0,) * len(shape))
    return pl.pallas_call(
        functools.partial(_prompt_mixer_kernel, tq=tq),
        grid=(batch, ns),
        in_specs=[
            pl.BlockSpec((None, tq, d), lambda b, s: (b, s, 0)),
            full((1, d)), full((d // 2, D_IN)), full((SWA_Q_HEADS, 1)), full((d // 2, d)),
            full((1, d)),
            full((1, LANES)),
        ],
        out_specs=[
            pl.BlockSpec((tq, d), lambda b, s: (b * ns + s, 0)),
            pl.BlockSpec((d // 2, tq), lambda b, s: (0, b * ns + s)),
            pl.BlockSpec((None, RET_HEADS, 128, 128), lambda b, s: (b, 0, 0, 0)),
            pl.BlockSpec((None, WINDOW, 128), lambda b, s: (b, 0, 0)),
            pl.BlockSpec((None, WINDOW, 128), lambda b, s: (b, 0, 0)),
        ],
        out_shape=[
            jax.ShapeDtypeStruct((batch * seq, d), F32),
            jax.ShapeDtypeStruct((d // 2, batch * seq), jnp.uint32),
            jax.ShapeDtypeStruct((batch, RET_HEADS, 128, 128), F32),
            jax.ShapeDtypeStruct((batch, WINDOW, 128), F32),
            jax.ShapeDtypeStruct((batch, WINDOW, 128), F32),
        ],
        scratch_shapes=[
            pltpu.VMEM((tq, D_IN), F32),
            pltpu.VMEM((RET_HEADS, 128, 128), F32),
            pltpu.VMEM((4, 2 * CHUNK, LANES), BF16),
            pltpu.VMEM((4, 2 * CHUNK, LANES), BF16),
            pltpu.VMEM((tq, d), BF16),
            pltpu.VMEM((RET_HEADS, CHUNK, CHUNK), F32),
            pltpu.VMEM((RET_HEADS, CHUNK, CHUNK), F32),
            pltpu.VMEM((RET_HEADS, CHUNK, CHUNK), F32),
        ],
        compiler_params=pltpu.CompilerParams(
            dimension_semantics=("arbitrary", "arbitrary"),
            vmem_limit_bytes=VMEM_LIMIT_BYTES),
        name="prompt_mixer",
    )(x, g1, win, sinks, wout, g2, inv)


def _sample_proj_kernel(x_ref, g1_ref, win_ref, inv_ref, proj_ref):
    xn = _rms_norm(x_ref[...], g1_ref[...]).astype(BF16)
    proj = jnp.dot(xn, pltpu.bitcast(win_ref[...], BF16), preferred_element_type=F32)
    proj_ref[...] = proj
    pos = jnp.full((1, 1), float(PAST_LEN), F32)
    cos, sin_signed, even = _rotation_tables(pos * jnp.ones((1, LANES), F32), inv_ref[...])
    even_full = jnp.broadcast_to(even, (x_ref.shape[0], LANES))
    for hh in range(RET_HEADS):
        ql = slice(OFF_RQ + hh * 128, OFF_RQ + (hh + 1) * 128)
        kl = slice(OFF_RK + hh * 128, OFF_RK + (hh + 1) * 128)
        proj_ref[:, ql] = _rotate_pairs(proj[:, ql], cos, sin_signed, even_full)
        proj_ref[:, kl] = _rotate_pairs(proj[:, kl], cos, sin_signed, even_full) * (RET_DK ** -0.5)


def _sample_proj(x, g1, win, inv):
    n = x.shape[0]
    return pl.pallas_call(
        _sample_proj_kernel,
        out_shape=jax.ShapeDtypeStruct((n, D_IN), F32),
        compiler_params=pltpu.CompilerParams(vmem_limit_bytes=VMEM_LIMIT_BYTES),
        name="sample_proj",
    )(x, g1, win, inv)


def _sample_state_kernel(proj_ref, st_ref, ck_ref, cv_ref, sinks_ref,
                         stn_ref, kn_ref, vn_ref, o2_ref, sres_ref, *, group):
    row8 = lax.broadcasted_iota(jnp.int32, (8, LANES), 0)
    lo1 = lax.broadcasted_iota(jnp.int32, (1, LANES), 1) < 64
    sink_col = sinks_ref[...]

    def sample_body(b, carry):
        row = pl.ds(b, 1)
        for hh in range(RET_HEADS):
            gamma = math.exp(LOG_DECAY[hh])
            q = proj_ref[row, OFF_RQ + hh * 128:OFF_RQ + (hh + 1) * 128]
            k = proj_ref[row, OFF_RK + hh * 128:OFF_RK + (hh + 1) * 128]
            v = proj_ref[row, OFF_RV + hh * 128:OFF_RV + (hh + 1) * 128]
            state = st_ref[b, hh]
            q8 = jnp.broadcast_to(q, (8, LANES)).astype(BF16)
            o2 = jnp.dot(q8, state.astype(BF16), preferred_element_type=F32)
            o2_ref[row, hh * 128:(hh + 1) * 128] = o2[0:1, :]
            k8 = jnp.where(row8 == 0, jnp.broadcast_to(k, (8, LANES)), 0.0).astype(BF16)
            v8 = jnp.broadcast_to(v, (8, LANES)).astype(BF16)
            outer = lax.dot_general(k8, v8, TN_DIMS, preferred_element_type=F32)
            stn_ref[b, hh] = state * gamma + outer
        kn_ref[b, 0:WINDOW - 1, :] = ck_ref[b, 1:WINDOW, :]
        vn_ref[b, 0:WINDOW - 1, :] = cv_ref[b, 1:WINDOW, :]
        kn_ref[b, WINDOW - 1:WINDOW, :] = proj_ref[row, OFF_SK:OFF_SK + 128]
        vn_ref[b, WINDOW - 1:WINDOW, :] = proj_ref[row, OFF_SV:OFF_SV + 128]
        kwin = kn_ref[b].astype(BF16)
        vwin = vn_ref[b].astype(BF16)
        q_pad = jnp.zeros((8, LANES), F32)
        for qb_idx in range(SWA_Q_HEADS // 2):
            g = qb_idx // 2
            blk = proj_ref[row, OFF_SQ + qb_idx * 128:OFF_SQ + (qb_idx + 1) * 128]
            rolled = pltpu.roll(blk, 64, 1)
            zero = jnp.zeros_like(blk)
            if g == 0:
                first, second = jnp.where(lo1, blk, zero), jnp.where(lo1, rolled, zero)
            else:
                first, second = jnp.where(lo1, zero, rolled), jnp.where(lo1, zero, blk)
            q_pad = jnp.where(row8 == 2 * qb_idx, jnp.broadcast_to(first, (8, LANES)), q_pad)
            q_pad = jnp.where(row8 == 2 * qb_idx + 1, jnp.broadcast_to(second, (8, LANES)), q_pad)
        sc = lax.dot_general(q_pad.astype(BF16), kwin, NT_DIMS,
                             preferred_element_type=F32) * (SWA_HD ** -0.5)
        m = jnp.maximum(jnp.max(sc, axis=-1, keepdims=True), sink_col)
        p = jnp.exp(sc - m)
        den = jnp.sum(p, axis=-1, keepdims=True) + jnp.exp(sink_col - m)
        sres_ref[b] = jnp.dot(p.astype(BF16), vwin, preferred_element_type=F32) / den
        return carry

    for b in range(group):
        sample_body(b, 0)


def _sample_state(proj, state, ck, cv, sinks, *, group):
    n = proj.shape[0]
    return pl.pallas_call(
        functools.partial(_sample_state_kernel, group=group),
        grid=(n // group,),
        in_specs=[
            pl.BlockSpec((group, D_IN), lambda i: (i, 0)),
            pl.BlockSpec((group, RET_HEADS, 128, 128), lambda i: (i, 0, 0, 0)),
            pl.BlockSpec((group, WINDOW, 128), lambda i: (i, 0, 0)),
            pl.BlockSpec((group, WINDOW, 128), lambda i: (i, 0, 0)),
            pl.BlockSpec((SWA_Q_HEADS, 1), lambda i: (0, 0)),
        ],
        out_specs=[
            pl.BlockSpec((group, RET_HEADS, 128, 128), lambda i: (i, 0, 0, 0)),
            pl.BlockSpec((group, WINDOW, 128), lambda i: (i, 0, 0)),
            pl.BlockSpec((group, WINDOW, 128), lambda i: (i, 0, 0)),
            pl.BlockSpec((group, 512), lambda i: (i, 0)),
            pl.BlockSpec((group, SWA_Q_HEADS, 128), lambda i: (i, 0, 0)),
        ],
        out_shape=[
            jax.ShapeDtypeStruct((n, RET_HEADS, 128, 128), F32),
            jax.ShapeDtypeStruct((n, WINDOW, 128), F32),
            jax.ShapeDtypeStruct((n, WINDOW, 128), F32),
            jax.ShapeDtypeStruct((n, 512), F32),
            jax.ShapeDtypeStruct((n, SWA_Q_HEADS, 128), F32),
        ],
        compiler_params=pltpu.CompilerParams(
            dimension_semantics=("arbitrary",), vmem_limit_bytes=VMEM_LIMIT_BYTES),
        name="sample_state",
    )(proj, state, ck, cv, sinks)


def _sample_finish_kernel(x_ref, proj_ref, o2_ref, sres_ref, wout_ref, g2_ref,
                          h_ref, h2t_ref, mix_ref):
    n = x_ref.shape[0]
    for hh in range(RET_HEADS):
        gamma = math.exp(LOG_DECAY[hh])
        lanes = slice(hh * 128, (hh + 1) * 128)
        q = proj_ref[:, OFF_RQ + hh * 128:OFF_RQ + (hh + 1) * 128]
        k = proj_ref[:, OFF_RK + hh * 128:OFF_RK + (hh + 1) * 128]
        v = proj_ref[:, OFF_RV + hh * 128:OFF_RV + (hh + 1) * 128]
        gate = proj_ref[:, OFF_RG + hh * 128:OFF_RG + (hh + 1) * 128]
        qk = jnp.sum(q * k, axis=-1, keepdims=True)
        o = qk * v + o2_ref[:, lanes] * gamma
        o = o * lax.rsqrt(jnp.mean(o * o, axis=-1, keepdims=True) + RMS_EPS)
        mix_ref[:, lanes] = (o * _silu(gate)).astype(BF16)
    lo = lax.broadcasted_iota(jnp.int32, (n, LANES), 1) < 64
    for qb_idx in range(SWA_Q_HEADS // 2):
        r0 = sres_ref[:, (2 * qb_idx) * 128:(2 * qb_idx + 1) * 128]
        r1 = sres_ref[:, (2 * qb_idx + 1) * 128:(2 * qb_idx + 2) * 128]
        if qb_idx // 2 == 0:
            blk = jnp.where(lo, r0, pltpu.roll(r1, 64, 1))
        else:
            blk = jnp.where(lo, pltpu.roll(r0, 64, 1), r1)
        mix_ref[:, 512 + qb_idx * 128:512 + (qb_idx + 1) * 128] = blk.astype(BF16)
    h = x_ref[...] + jnp.dot(mix_ref[...], pltpu.bitcast(wout_ref[...], BF16),
                             preferred_element_type=F32)
    h_ref[...] = h
    h2t_ref[...] = pltpu.bitcast(_rms_norm(h, g2_ref[...]).T.astype(BF16), jnp.uint32)


def _sample_finish(x, proj, o2, sres, wout, g2):
    n, d = x.shape
    return pl.pallas_call(
        _sample_finish_kernel,
        out_shape=[jax.ShapeDtypeStruct((n, d), F32),
                   jax.ShapeDtypeStruct((d // 2, n), jnp.uint32)],
        scratch_shapes=[pltpu.VMEM((n, d), BF16)],
        compiler_params=pltpu.CompilerParams(vmem_limit_bytes=VMEM_LIMIT_BYTES),
        name="sample_finish",
    )(x, proj, o2, sres, wout, g2)


def _sorting_network(n):
    def merge(lo, hi, r):
        step = r * 2
        if step < hi - lo:
            yield from merge(lo, hi, step)
            yield from merge(lo + r, hi, step)
            yield from [(i, i + r) for i in range(lo + r, hi - r, step)]
        else:
            yield (lo, lo + r)

    def sort(lo, hi):
        if hi - lo >= 1:
            mid = lo + (hi - lo) // 2
            yield from sort(lo, mid)
            yield from sort(mid + 1, hi)
            yield from merge(lo, hi, 1)

    size = 1
    while size < n:
        size *= 2
    return tuple((i, j) for i, j in sort(0, size - 1) if j < n)


def _top_values(lists, count):
    lists = list(lists)
    for i, j in _sorting_network(len(lists)):
        lists[i], lists[j] = jnp.maximum(lists[i], lists[j]), jnp.minimum(lists[i], lists[j])
    rows = []
    for p in range(count):
        m = jnp.max(lists[0], axis=0, keepdims=True)
        rows.append(m)
        if p + 1 < count:
            hit = lists[0] == m
            depth = min(len(lists), count - p)
            for k in range(depth - 1):
                lists[k] = jnp.where(hit, lists[k + 1], lists[k])
            if depth == len(lists):
                lists[depth - 1] = jnp.where(hit, NEG_INF, lists[depth - 1])
    return rows


def _sublane_lists(s):
    return [s[k * 8:(k + 1) * 8, :] for k in range(s.shape[0] // 8)]


def _stack8(rows, row8):
    out = jnp.zeros(row8.shape, F32)
    for r, v in enumerate(rows):
        out = jnp.where(row8 == r, jnp.broadcast_to(v, row8.shape), out)
    return out


def _routing_tables(s0, s1):
    a = _top_values(_sublane_lists(s0), PEER_TOPK)
    b = _top_values(_sublane_lists(s1), PEER_TOPK)
    row8 = lax.broadcasted_iota(jnp.int32, (8, s0.shape[1]), 0)
    a_lo, a_hi = _stack8(a[:8], row8), _stack8(a[8:], row8)
    b_lo, b_hi = _stack8(b[:8], row8), _stack8(b[8:], row8)
    neg = jnp.full(row8.shape, NEG_INF, F32)
    cands = [
        a[0] + b_lo, a[0] + b_hi, a[1] + b_lo,
        jnp.where(row8 < 5, a[2] + b_lo, neg),
        jnp.where(row8 < 4, a[3] + b_lo, neg),
        b[0] + a_hi,
        jnp.where(row8 >= 4, b[0] + a_lo, neg),
        jnp.where(row8 >= 4, b[1] + a_lo, neg),
        jnp.where(row8 == 4, b[2] + a_lo, neg),
    ]
    tau = _top_values(cands, PEER_TOPK)[-1]
    top = a[0] + b[0]
    passed = [c >= tau for c in cands]
    col_sum = lambda x: jnp.sum(x, axis=0, keepdims=True)
    z = functools.reduce(lambda x, y: x + y, [
        col_sum(jnp.where(ok, jnp.exp(c - top), 0.0)) for ok, c in zip(passed, cands)])
    cnt = [jnp.where(ok, 1.0, 0.0) for ok in passed]
    by_row = cnt[6] + cnt[7] + cnt[8]
    n = [col_sum(cnt[0]) + col_sum(cnt[1]), col_sum(cnt[2]), col_sum(cnt[3]), col_sum(cnt[4])]
    n += [by_row[p:p + 1, :] for p in range(4, 8)]
    n += [cnt[5][p:p + 1, :] for p in range(8)]
    n0 = jnp.zeros(s0.shape, F32)
    rank1 = jnp.zeros(s1.shape, F32)
    for p in range(PEER_TOPK):
        n0 = jnp.where(s0 == a[p], n[p], n0)
        rank1 = jnp.where(s1 < b[p], p + 1.0, rank1)
    e1 = jnp.exp(s1 - b[0])
    e0n = jnp.exp(s0 - a[0]) * (1.0 / z)
    return rank1, e1, n0, e0n


def _routing_kernel(h2t_ref, wqt_ref, keys_ref, r1_ref, e1_ref, n0_ref, e0_ref,
                    q_ref, s_ref, *, tt):
    q_ref[...] = jnp.dot(pltpu.bitcast(wqt_ref[...], BF16), pltpu.bitcast(h2t_ref[...], BF16),
                         preferred_element_type=F32)
    n_lg = tt // LANES

    def head_body(h, carry):
        for c in range(2):
            r0 = pl.multiple_of((2 * h + c) * 128, 128)
            q_hc = q_ref[pl.ds(r0, 128), :].astype(BF16)
            key_words = keys_ref[pl.ds(pl.multiple_of((2 * h + c) * KEY_WORDS, KEY_WORDS),
                                       KEY_WORDS), :]
            s_ref[c] = jnp.dot(pltpu.bitcast(key_words, BF16), q_hc, preferred_element_type=F32)
        out_rows = pl.ds(pl.multiple_of(h * 128, 128), 128)
        packed_rows = pl.ds(pl.multiple_of(h * KEY_WORDS, KEY_WORDS), KEY_WORDS)
        rank1, e1, n0, e0n = _routing_tables(s_ref[0], s_ref[1])
        for lg in range(n_lg):
            lanes = slice(lg * LANES, (lg + 1) * LANES)
            r1_ref[lg, packed_rows, :] = pltpu.bitcast(rank1[:, lanes].astype(BF16), jnp.uint32)
            e1_ref[lg, packed_rows, :] = pltpu.bitcast(e1[:, lanes].astype(BF16), jnp.uint32)
            n0_ref[lg, out_rows, :] = n0[:, lanes]
            e0_ref[lg, out_rows, :] = e0n[:, lanes]
        return carry

    lax.fori_loop(0, PEER_HEADS, head_body, 0)


def _routing(h2t, wqt, keys, *, tt):
    d, t = 2 * h2t.shape[0], h2t.shape[1]
    rows = PEER_HEADS * PEER_N_KEYS
    tab = lambda r, dt: jax.ShapeDtypeStruct((t // LANES, r, LANES), dt)
    tab_spec = lambda r: pl.BlockSpec((tt // LANES, r, LANES), lambda i: (i, 0, 0))
    return pl.pallas_call(
        functools.partial(_routing_kernel, tt=tt),
        grid=(t // tt,),
        in_specs=[
            pl.BlockSpec((d // 2, tt), lambda i: (0, i)),
            pl.BlockSpec(wqt.shape, lambda i: (0, 0)),
            pl.BlockSpec(keys.shape, lambda i: (0, 0)),
        ],
        out_specs=[tab_spec(rows // 2), tab_spec(rows // 2), tab_spec(rows), tab_spec(rows)],
        out_shape=[tab(rows // 2, jnp.uint32), tab(rows // 2, jnp.uint32),
                   tab(rows, F32), tab(rows, F32)],
        scratch_shapes=[pltpu.VMEM((2 * wqt.shape[0], tt), F32), pltpu.VMEM((2, 128, tt), F32)],
        compiler_params=pltpu.CompilerParams(
            dimension_semantics=("arbitrary",), vmem_limit_bytes=VMEM_LIMIT_BYTES),
        name="peer_routing",
    )(h2t, wqt, keys)


def _experts_kernel(xt_ref, u_ref, vt_ref, r1_ref, e1_ref, n0_ref, e0_ref, h_ref, fg_ref,
                    y_ref, acc_ref, hd_ref, at_ref, *, tt, et, n_e, n_tiles):
    g = pl.program_id(0)
    e_c = jnp.clip(g - 2, 0, n_tiles - 1) % n_e

    @pl.when(g == 0)
    def _prime():
        hd_ref[...] = jnp.zeros_like(hd_ref)
        at_ref[...] = jnp.zeros_like(at_ref)

    @pl.when(e_c == 0)
    def _new_token_tile():
        acc_ref[...] = jnp.zeros_like(acc_ref)

    stages = functools.partial(_experts_stages, xt_ref, u_ref, vt_ref, r1_ref, e1_ref, n0_ref,
                               e0_ref, acc_ref, hd_ref, at_ref, tt=tt, et=et, n_e=n_e,
                               n_tiles=n_tiles)
    pl.when(g % 2 == 0)(functools.partial(stages, slot_a=0))
    pl.when(g % 2 == 1)(functools.partial(stages, slot_a=1))

    @pl.when((g >= 2) & (e_c == n_e - 1))
    def _finish():
        y_ref[...] = _rms_norm(h_ref[...] + acc_ref[...].T, fg_ref[...])


def _experts_stages(xt_ref, u_ref, vt_ref, r1_ref, e1_ref, n0_ref, e0_ref, acc_ref, hd_ref,
                    at_ref, *, tt, et, n_e, n_tiles, slot_a):
    slot_b = 1 - slot_a
    n_ib = et // PEER_N_KEYS
    sub = 16
    e_b = jnp.clip(pl.program_id(0) - 1, 0, n_tiles - 1) % n_e

    def packed(rows):
        return slice(rows.start // 2, rows.stop // 2)

    def stage_a(rows):
        u = pltpu.bitcast(u_ref[packed(rows), :], BF16)
        xt = pltpu.bitcast(xt_ref[...], BF16)
        hd_ref[slot_a, rows, :] = jnp.dot(u, xt, preferred_element_type=F32)

    def stage_b(ii_pair, lg):
        lanes = slice(lg * LANES, (lg + 1) * LANES)
        firsts = tuple(range(FIRST_KEYS_PER_PIECE * ii_pair, FIRST_KEYS_PER_PIECE * (ii_pair + 1)))
        w = [jnp.zeros((PEER_N_KEYS, LANES), BF16) for _ in firsts]
        for h in range(PEER_HEADS):
            r1 = pltpu.bitcast(r1_ref[lg, h * KEY_WORDS:(h + 1) * KEY_WORDS, :], BF16)
            e1 = pltpu.bitcast(e1_ref[lg, h * KEY_WORDS:(h + 1) * KEY_WORDS, :], BF16)
            grp = pl.ds(pl.multiple_of(h * 128 + e_b * n_ib + (firsts[0] // 8) * 8, 8), 8)
            n0_rows, e0_rows = n0_ref[lg, grp, :], e0_ref[lg, grp, :]
            for n, ii in enumerate(firsts):
                n0b = jnp.broadcast_to(n0_rows[ii % 8:ii % 8 + 1, :], (sub, LANES))
                e0b = jnp.broadcast_to(e0_rows[ii % 8:ii % 8 + 1, :], (sub, LANES))
                n0b = jnp.concatenate([n0b.astype(BF16)] * (PEER_N_KEYS // sub), axis=0)
                e0b = jnp.concatenate([e0b.astype(BF16)] * (PEER_N_KEYS // sub), axis=0)
                w[n] = w[n] + jnp.where(r1 < n0b, e1 * e0b, jnp.zeros_like(e1))
        for n, ii in enumerate(firsts):
            rows = slice(ii * 128, (ii + 1) * 128)
            x = hd_ref[slot_b, rows, lanes]
            gelu = 0.5 * x * (1.0 + lax.erf(x * math.sqrt(0.5)))
            at_ref[slot_b, rows, lanes] = gelu.astype(BF16) * w[n]

    def stage_c(rows):
        vt = pltpu.bitcast(vt_ref[packed(rows), :], BF16)
        acc_ref[rows, :] += jnp.dot(vt, at_ref[slot_a], preferred_element_type=F32)

    d = acc_ref.shape[0]
    n_lg = tt // LANES
    b_pieces = [(ii_pair, lg) for ii_pair in range(n_ib // FIRST_KEYS_PER_PIECE)
                for lg in range(n_lg)]
    pieces = sorted([((k + 0.5) * MXU_ROWS / et, 0, k) for k in range(et // MXU_ROWS)]
                    + [((k + 0.5) * MXU_ROWS / d, 1, k) for k in range(d // MXU_ROWS)])
    done = 0
    for idx, (_, is_c, k) in enumerate(pieces):
        upto = int((idx + 0.5) * len(b_pieces) / len(pieces))
        for piece in b_pieces[done:upto]:
            stage_b(*piece)
        done = max(done, upto)
        (stage_c if is_c else stage_a)(slice(k * MXU_ROWS, (k + 1) * MXU_ROWS))
    for piece in b_pieces[done:]:
        stage_b(*piece)


def _experts(xt, u, vt, r1, e1, n0, e0, h, fg, *, tt, et):
    d, t = 2 * xt.shape[0], xt.shape[1]
    n_exp = 2 * u.shape[0]
    rows = n0.shape[1]
    assert et % (8 * PEER_N_KEYS) == 0 and t % tt == 0 and n_exp % et == 0
    assert 8 % FIRST_KEYS_PER_PIECE == 0
    n_e = n_exp // et
    n_tiles = (t // tt) * n_e
    tile_a = lambda g: jnp.minimum(g, n_tiles - 1)
    tile_b = lambda g: jnp.clip(g - 1, 0, n_tiles - 1)
    tile_c = lambda g: jnp.clip(g - 2, 0, n_tiles - 1)
    tab_spec = lambda r: pl.BlockSpec((tt // LANES, r, LANES), lambda g: (tile_b(g) // n_e, 0, 0))
    return pl.pallas_call(
        functools.partial(_experts_kernel, tt=tt, et=et, n_e=n_e, n_tiles=n_tiles),
        grid=(n_tiles + 2,),
        in_specs=[
            pl.BlockSpec((d // 2, tt), lambda g: (0, tile_a(g) // n_e)),
            pl.BlockSpec((et // 2, d), lambda g: (tile_a(g) % n_e, 0)),
            pl.BlockSpec((d // 2, et), lambda g: (0, tile_c(g) % n_e)),
            tab_spec(rows // 2), tab_spec(rows // 2), tab_spec(rows), tab_spec(rows),
            pl.BlockSpec((tt, d), lambda g: (tile_c(g) // n_e, 0)),
            pl.BlockSpec((1, d), lambda g: (0, 0)),
        ],
        out_specs=pl.BlockSpec((tt, d), lambda g: (tile_c(g) // n_e, 0)),
        out_shape=jax.ShapeDtypeStruct((t, d), F32),
        scratch_shapes=[
            pltpu.VMEM((d, tt), F32),
            pltpu.VMEM((2, et, tt), F32),
            pltpu.VMEM((2, et, tt), BF16),
        ],
        compiler_params=pltpu.CompilerParams(
            dimension_semantics=("arbitrary",),
            vmem_limit_bytes=VMEM_LIMIT_BYTES),
        name="peer_experts",
    )(xt, u, vt, r1, e1, n0, e0, h, fg)


def _pack_kernel(x_ref, o_ref, *, transpose):
    x = x_ref[...]
    if transpose:
        x = x.T
    o_ref[...] = pltpu.bitcast(x.astype(BF16), jnp.uint32)


def _pack_bf16(x, *, transpose=False, block_rows=1024):
    m, n = x.shape
    bm = min(block_rows, m)
    assert m % bm == 0
    if transpose:
        out_shape, out_spec = (n // 2, m), pl.BlockSpec((n // 2, bm), lambda i: (0, i))
    else:
        out_shape, out_spec = (m // 2, n), pl.BlockSpec((bm // 2, n), lambda i: (i, 0))
    return pl.pallas_call(
        functools.partial(_pack_kernel, transpose=transpose),
        grid=(m // bm,),
        in_specs=[pl.BlockSpec((bm, n), lambda i: (i, 0))],
        out_specs=out_spec,
        out_shape=jax.ShapeDtypeStruct(out_shape, jnp.uint32),
        compiler_params=pltpu.CompilerParams(
            dimension_semantics=("arbitrary",), vmem_limit_bytes=VMEM_LIMIT_BYTES),
        name="pack_bf16",
    )(x)


def _peer_and_final(h, h2t, wqt, keys, u, vt, fg, *, tt_route, tt, et):
    r1, e1, n0, e0 = _routing(h2t, wqt, keys, tt=tt_route)
    return _experts(h2t, u, vt, r1, e1, n0, e0, h, fg, tt=tt, et=et)


def kernel(x_prompt, x_sample, state_ret, cache_swa_k, cache_swa_v, norm1_g, w_in, swa_sinks, w_out, norm2_g, peer_w_q, peer_sub_keys, peer_u, peer_v, final_g):
    batch, seq, d = x_prompt.shape
    n_s = x_sample.shape[0]
    g1 = norm1_g.reshape(1, d)
    g2 = norm2_g.reshape(1, d)
    fg = final_g.reshape(1, d)
    sinks = swa_sinks.astype(F32).reshape(SWA_Q_HEADS, 1)
    win = _pack_bf16(w_in)
    wout = _pack_bf16(w_out)
    wqt = _pack_bf16(peer_w_q, transpose=True)
    keys = _pack_bf16(peer_sub_keys.reshape(PEER_HEADS * 2 * PEER_N_KEYS, -1))
    u = _pack_bf16(peer_u)
    vt = _pack_bf16(peer_v, transpose=True)
    half = RET_DK // 2
    inv = 1.0 / (ROPE_BASE ** jnp.linspace(0.0, 1.0, half, dtype=F32))
    inv = jnp.repeat(inv, 2).reshape(1, RET_DK)

    h_p, h2t_p, st_p, k_p, v_p = _prompt_mixer(x_prompt, g1, win, sinks, wout, g2, inv,
                                               tq=PROMPT_MIXER_TOKENS)
    y_p = _peer_and_final(h_p, h2t_p, wqt, keys, u, vt, fg, tt_route=PROMPT_ROUTING_TOKENS,
                          tt=PROMPT_EXPERT_TOKENS, et=PROMPT_EXPERTS_PER_STEP)

    xs = x_sample.reshape(n_s, d)
    proj = _sample_proj(xs, g1, win, inv)
    st_s, k_s, v_s, o2, sres = _sample_state(
        proj, state_ret, cache_swa_k.reshape(n_s, WINDOW, 128),
        cache_swa_v.reshape(n_s, WINDOW, 128), sinks, group=SAMPLE_STATE_GROUP)
    h_s, h2t_s = _sample_finish(xs, proj, o2, sres.reshape(n_s, SWA_Q_HEADS * 128), wout, g2)
    y_s = _peer_and_final(h_s, h2t_s, wqt, keys, u, vt, fg, tt_route=n_s, tt=n_s,
                          et=SAMPLE_EXPERTS_PER_STEP)

    kv_shape = (WINDOW, 2, SWA_HD)
    return (y_p.reshape(batch, seq, d), y_s.reshape(n_s, 1, d), st_p,
            k_p.reshape(batch, *kv_shape), v_p.reshape(batch, *kv_shape),
            st_s, k_s.reshape(n_s, *kv_shape), v_s.reshape(n_s, *kv_shape))
```
